```python
import math
import jax
import jax.numpy as jnp
from jax import lax
import numpy as np

D_MODEL = 1024
BATCH = 32
SEQ = 2048
DEPTH = 4

BRANCH_WIDTH = 512
N_BRANCH = 3
CHUNK = 128
GLA_CHUNK = 64
SSM_INNER = BRANCH_WIDTH
SSM_HEAD_DIM = 64
SSM_HEADS = SSM_INNER // SSM_HEAD_DIM
SSM_GROUPS = 2
SSM_STATE = 64
SSM_CONV = 4
SSM_CONV_DIM = SSM_INNER + 2 * SSM_GROUPS * SSM_STATE
RET_HEADS = 8
RET_HEAD_DIM = BRANCH_WIDTH // RET_HEADS
RET_DIM = RET_HEADS * RET_HEAD_DIM
ROPE_BASE = 10000.0
GLA_HEADS = 4
GLA_KEY = BRANCH_WIDTH // 2
GLA_VAL = BRANCH_WIDTH
GLA_KEY_HEAD = GLA_KEY // GLA_HEADS
GLA_VAL_HEAD = GLA_VAL // GLA_HEADS
GLA_RANK = 16
GLA_GATE_NORMALIZER = 16.0
D_FF = 2816
FFN_CONV = 3
IN_SIZES = (SSM_INNER, SSM_CONV_DIM, SSM_HEADS,
            RET_DIM, RET_DIM, RET_DIM, RET_DIM,
            GLA_KEY, GLA_KEY, GLA_VAL, GLA_VAL, GLA_RANK,
            N_BRANCH * D_MODEL)
N_IN = 1288 + 2048 + 1552 + 3072
RMS_EPS = 1e-6
GN_EPS = 1e-5

kernel_name = "hybrid_ssd_retention_gla_convffn"


def rms_norm(x, g):
    xf = x.astype(jnp.float32)
    y = xf * lax.rsqrt(jnp.mean(xf * xf, axis=-1, keepdims=True) + RMS_EPS)
    return y.astype(x.dtype) * g


def grouped_norm(x, g, groups, center):
    bsz, seq, w = x.shape
    xf = x.astype(jnp.float32).reshape(bsz, seq, groups, w // groups)
    if center:
        xf = xf - jnp.mean(xf, axis=-1, keepdims=True)
    y = xf * lax.rsqrt(jnp.mean(xf * xf, axis=-1, keepdims=True) + GN_EPS)
    return y.reshape(bsz, seq, w).astype(x.dtype) * g


def causal_dwconv(x, w, b):
    k, c = w.shape
    y = lax.conv_general_dilated(x, w[:, None, :], window_strides=(1,), padding=[(k - 1, 0)],
                                 dimension_numbers=('NWC', 'WIO', 'NWC'), feature_group_count=c)
    return y + b


def rotary(x, positions):
    half = x.shape[-1] // 2
    inv_freq = ROPE_BASE ** (-jnp.arange(half, dtype=jnp.float32) / half)
    ang = positions.astype(jnp.float32)[:, None] * inv_freq[None, :]
    cos = jnp.cos(ang)[None, :, None, :]
    sin = jnp.sin(ang)[None, :, None, :]
    x1 = x[..., :half].astype(jnp.float32)
    x2 = x[..., half:].astype(jnp.float32)
    return jnp.concatenate([x1 * cos - x2 * sin, x1 * sin + x2 * cos], axis=-1).astype(x.dtype)


def chunked_scalar_decay(q, k, v, log_a, chunk):
    bsz, seq, nh, dn = q.shape
    dp = v.shape[-1]
    nc = seq // chunk
    dt = q.dtype
    q = q.reshape(bsz, nc, chunk, nh, dn)
    k = k.reshape(bsz, nc, chunk, nh, dn)
    v = v.reshape(bsz, nc, chunk, nh, dp)
    cum = jnp.cumsum(log_a.astype(jnp.float32).reshape(bsz, nc, chunk, nh), axis=2)
    cum_t = jnp.swapaxes(cum, 2, 3)
    causal = jnp.tril(jnp.ones((chunk, chunk), dtype=bool))
    seg = cum_t[..., :, None] - cum_t[..., None, :]
    decay = jnp.exp(jnp.where(causal, seg, -jnp.inf)).astype(dt)
    scores = jnp.einsum('bclhn,bcshn->bchls', q, k) * decay
    y_intra = jnp.einsum('bchls,bcshp->bclhp', scores, v)
    last = cum_t[..., -1:]
    w_state = jnp.exp(last - cum_t).astype(dt)
    chunk_kv = jnp.einsum('bcshn,bchs,bcshp->bchnp', k, w_state, v)
    chunk_decay = jnp.exp(last[..., 0]).astype(chunk_kv.dtype)

    def step(state, inp):
        kv_c, dec_c = inp
        return dec_c[..., None, None] * state + kv_c, state

    init = jnp.zeros((bsz, nh, dn, dp), dtype=chunk_kv.dtype)
    _, s_prev = lax.scan(step, init, (jnp.moveaxis(chunk_kv, 1, 0), jnp.moveaxis(chunk_decay, 1, 0)))
    s_prev = jnp.moveaxis(s_prev, 0, 1)
    q_dec = q * jnp.exp(cum)[..., None].astype(dt)
    y_inter = jnp.einsum('bclhn,bchnp->bclhp', q_dec, s_prev)
    return (y_intra + y_inter).reshape(bsz, seq, nh, dp)


def chunked_gla(q, k, v, log_alpha, chunk):
    bsz, seq, nh, dn = q.shape
    dp = v.shape[-1]
    nc = seq // chunk
    dt = q.dtype
    q = q.reshape(bsz, nc, chunk, nh, dn)
    k = k.reshape(bsz, nc, chunk, nh, dn)
    v = v.reshape(bsz, nc, chunk, nh, dp)
    cum = jnp.cumsum(log_alpha.astype(jnp.float32).reshape(bsz, nc, chunk, nh, dn), axis=2)
    last = cum[:, :, -1:]
    q_dec = (q * jnp.exp(cum)).astype(dt)
    k_inv = (k * jnp.exp(-cum)).astype(dt)
    k_state = (k * jnp.exp(last - cum)).astype(dt)
    causal = jnp.tril(jnp.ones((chunk, chunk), dtype=bool))
    scores = jnp.where(causal, jnp.einsum('bclhn,bcshn->bchls', q_dec, k_inv), 0.0).astype(dt)
    y_intra = jnp.einsum('bchls,bcshp->bclhp', scores, v)
    chunk_kv = jnp.einsum('bcshn,bcshp->bchnp', k_state, v)
    chunk_decay = jnp.exp(last[:, :, 0]).astype(chunk_kv.dtype)

    def step(state, inp):
        kv_c, dec_c = inp
        return dec_c[..., None] * state + kv_c, state

    init = jnp.zeros((bsz, nh, dn, dp), dtype=chunk_kv.dtype)
    _, s_prev = lax.scan(step, init, (jnp.moveaxis(chunk_kv, 1, 0), jnp.moveaxis(chunk_decay, 1, 0)))
    s_prev = jnp.moveaxis(s_prev, 0, 1)
    y_inter = jnp.einsum('bclhn,bchnp->bclhp', q_dec, s_prev)
    return (y_intra + y_inter).reshape(bsz, seq, nh, dp)


def hybrid_mixer(h, w_in, ssm_conv_w, ssm_conv_b, ssm_dt_bias, ssm_a_log, ssm_d, ssm_norm_g,
                 ret_norm_g, gla_w_alpha2, gla_b_alpha, gla_norm_g, w_branch, b_gate, w_out):
    bsz, seq, _ = h.shape
    proj = h @ w_in
    idx = np.cumsum(np.array(IN_SIZES))[:-1].tolist()
    (z, xbc, dt_raw, r_q, r_k, r_v, r_g, g_q, g_k, g_v, g_r, g_lr, gate_logits) = jnp.split(proj, idx, axis=-1)

    xbc = jax.nn.silu(causal_dwconv(xbc, ssm_conv_w, ssm_conv_b))
    xs, bm, cm = jnp.split(xbc, [SSM_INNER, SSM_INNER + SSM_GROUPS * SSM_STATE], axis=-1)
    xs = xs.reshape(bsz, seq, SSM_HEADS, SSM_HEAD_DIM)
    heads_per_group = SSM_HEADS // SSM_GROUPS
    bh = jnp.repeat(bm.reshape(bsz, seq, SSM_GROUPS, SSM_STATE), heads_per_group, axis=2)
    ch = jnp.repeat(cm.reshape(bsz, seq, SSM_GROUPS, SSM_STATE), heads_per_group, axis=2)
    dt = jax.nn.softplus(dt_raw.astype(jnp.float32) + ssm_dt_bias.astype(jnp.float32))
    a = -jnp.exp(ssm_a_log.astype(jnp.float32))
    y_ssm = chunked_scalar_decay(ch, bh, xs * dt[..., None].astype(xs.dtype), dt * a, CHUNK)
    y_ssm = (y_ssm + ssm_d[:, None] * xs).reshape(bsz, seq, SSM_INNER)
    y_ssm = grouped_norm(y_ssm * jax.nn.silu(z), ssm_norm_g, SSM_GROUPS, center=False)

    positions = jnp.arange(seq)
    rq = rotary(r_q.reshape(bsz, seq, RET_HEADS, RET_HEAD_DIM), positions)
    rk = rotary(r_k.reshape(bsz, seq, RET_HEADS, RET_HEAD_DIM), positions) * (RET_HEAD_DIM ** -0.5)
    rv = r_v.reshape(bsz, seq, RET_HEADS, RET_HEAD_DIM)
    log_gamma = jnp.log(1.0 - jnp.exp2(-5.0 - jnp.arange(RET_HEADS, dtype=jnp.float32)))
    y_ret = chunked_scalar_decay(rq, rk, rv, jnp.broadcast_to(log_gamma, (bsz, seq, RET_HEADS)), CHUNK)
    y_ret = grouped_norm(y_ret.reshape(bsz, seq, RET_DIM), ret_norm_g, RET_HEADS, center=True)
    y_ret = jax.nn.silu(r_g) * y_ret

    gq = g_q.reshape(bsz, seq, GLA_HEADS, GLA_KEY_HEAD) * (GLA_KEY_HEAD ** -0.5)
    gk = g_k.reshape(bsz, seq, GLA_HEADS, GLA_KEY_HEAD)
    gv = g_v.reshape(bsz, seq, GLA_HEADS, GLA_VAL_HEAD)
    alpha_logits = (g_lr @ gla_w_alpha2 + gla_b_alpha).astype(jnp.float32)
    log_alpha = (jax.nn.log_sigmoid(alpha_logits) / GLA_GATE_NORMALIZER).reshape(bsz, seq, GLA_HEADS, GLA_KEY_HEAD)
    y_gla = chunked_gla(gq, gk, gv, log_alpha, GLA_CHUNK).reshape(bsz, seq, GLA_VAL)
    y_gla = jax.nn.silu(g_r) * grouped_norm(y_gla, gla_norm_g, GLA_HEADS, center=False)

    gates = jax.nn.sigmoid(gate_logits.reshape(bsz, seq, N_BRANCH, D_MODEL) + b_gate)
    merged = (gates[:, :, 0] * (y_ssm @ w_branch[0])
              + gates[:, :, 1] * (y_ret @ w_branch[1])
              + gates[:, :, 2] * (y_gla @ w_branch[2]))
    return merged @ w_out


def conv_ffn(h, w_up, conv_w, conv_b, w_down):
    u = causal_dwconv(h @ w_up, conv_w, conv_b)
    gate, val = jnp.split(u, 2, axis=-1)
    return (jax.nn.silu(gate) * val) @ w_down


def setup_inputs(seed: int = 0) -> dict:
    key = jax.random.key(seed)
    ks = jax.random.split(key, 24)
    L, D = DEPTH, D_MODEL
    nrm = lambda k, shape, scale: jax.random.normal(k, shape, jnp.float32) * scale
    gain = lambda k, shape: 1.0 + 0.02 * jax.random.normal(k, shape, jnp.float32)
    dt0 = jnp.exp(jax.random.uniform(ks[5], (L, SSM_HEADS), jnp.float32, math.log(1e-3), math.log(1e-1)))
    return {
        "x": jax.random.normal(ks[0], (BATCH, SEQ, D), jnp.float32),
        "norm_mix_g": gain(ks[1], (L, D)),
        "w_in": nrm(ks[2], (L, D, N_IN), D ** -0.5),
        "ssm_conv_w": nrm(ks[3], (L, SSM_CONV, SSM_CONV_DIM), SSM_CONV ** -0.5),
        "ssm_conv_b": nrm(ks[4], (L, SSM_CONV_DIM), 0.02),
        "ssm_dt_bias": dt0 + jnp.log(-jnp.expm1(-dt0)),
        "ssm_a_log": jnp.log(jax.random.uniform(ks[6], (L, SSM_HEADS), jnp.float32, 1.0, 16.0)),
        "ssm_d": gain(ks[7], (L, SSM_HEADS)),
        "ssm_norm_g": gain(ks[8], (L, SSM_INNER)),
        "ret_norm_g": gain(ks[9], (L, RET_DIM)),
        "gla_w_alpha2": nrm(ks[10], (L, GLA_RANK, GLA_KEY), GLA_RANK ** -0.5),
        "gla_b_alpha": nrm(ks[11], (L, GLA_KEY), 0.02),
        "gla_norm_g": gain(ks[12], (L, GLA_VAL)),
        "w_branch": nrm(ks[13], (L, N_BRANCH, BRANCH_WIDTH, D), BRANCH_WIDTH ** -0.5),
        "b_gate": nrm(ks[14], (L, N_BRANCH, D), 0.02),
        "w_out": nrm(ks[15], (L, D, D), D ** -0.5),
        "norm_ffn_g": gain(ks[16], (L, D)),
        "w_up": nrm(ks[17], (L, D, 2 * D_FF), D ** -0.5),
        "ffn_conv_w": nrm(ks[18], (L, FFN_CONV, 2 * D_FF), FFN_CONV ** -0.5),
        "ffn_conv_b": nrm(ks[19], (L, 2 * D_FF), 0.02),
        "w_down": nrm(ks[20], (L, D_FF, D), D_FF ** -0.5),
        "norm_f_g": gain(ks[21], (D,)),
    }


def reference(x, norm_mix_g, w_in, ssm_conv_w, ssm_conv_b, ssm_dt_bias, ssm_a_log, ssm_d, ssm_norm_g,
              ret_norm_g, gla_w_alpha2, gla_b_alpha, gla_norm_g, w_branch, b_gate, w_out,
              norm_ffn_g, w_up, ffn_conv_w, ffn_conv_b, w_down, norm_f_g):
    for i in range(DEPTH):
        h = rms_norm(x, norm_mix_g[i])
        x = x + hybrid_mixer(h, w_in[i], ssm_conv_w[i], ssm_conv_b[i], ssm_dt_bias[i], ssm_a_log[i],
                             ssm_d[i], ssm_norm_g[i], ret_norm_g[i], gla_w_alpha2[i], gla_b_alpha[i],
                             gla_norm_g[i], w_branch[i], b_gate[i], w_out[i])
        h = rms_norm(x, norm_ffn_g[i])
        x = x + conv_ffn(h, w_up[i], ffn_conv_w[i], ffn_conv_b[i], w_down[i])
    return rms_norm(x, norm_f_g)
```

```python
import functools
import math

import numpy as np
import jax
import jax.numpy as jnp
from jax import lax
from jax.experimental import pallas as pl
from jax.experimental.pallas import tpu as pltpu

F32 = jnp.float32
MXU_DTYPE = jnp.bfloat16

D_MODEL = 1024
BRANCH_WIDTH = 512
N_BRANCH = 3
CHUNK = 128
GLA_CHUNK = 64
SSM_HEADS = 8
SSM_CONV = 4
SSM_CONV_DIM = 768
RET_HEADS = 8
RET_HEAD_DIM = 64
ROPE_BASE = 10000.0
GLA_HEADS = 4
GLA_KEY = 256
GLA_RANK = 16
GLA_GATE_NORMALIZER = 16.0
D_FF = 2816
FFN_CONV = 3
RMS_EPS = 1e-6
GN_EPS = 1e-5

_OFF_Z = 0
_OFF_XBC = 512
_OFF_DT = 1280
_OFF_RET = 1288
_OFF_GLA = 3336
_OFF_GLR = 4872
_OFF_GATE = 4888
N_IN = 7960

LANES = 128
SUBLANES = 8
HALF = 64
MIX_TM = 256
FFN_TM = 512
FFN_BW = 256
VMEM_LIMIT_BYTES = 60000 * 1024
NEG_BIG = -1e30


def _sigmoid(x):
    return 1.0 / (1.0 + jnp.exp(-x))


def _silu(x):
    return x * _sigmoid(x)


def _softplus(x):
    return jnp.maximum(x, 0.0) + jnp.log(1.0 + jnp.exp(-jnp.abs(x)))


def _split2(x):
    hi = x.astype(MXU_DTYPE)
    lo = (x - hi.astype(F32)).astype(MXU_DTYPE)
    return hi, lo


def _const_dot(c2, x):
    hi, lo = _split2(x)
    return jnp.dot(c2, jnp.concatenate([hi, lo], axis=0), preferred_element_type=F32)


def _dot_const(x, c2):
    hi, lo = _split2(x)
    return jnp.dot(jnp.concatenate([hi, lo], axis=1), c2, preferred_element_type=F32)


def _dot(a, b):
    return jnp.dot(a, b, preferred_element_type=F32)


def _dot_nt(a, b):
    return lax.dot_general(a, b, (((1,), (1,)), ((), ())), preferred_element_type=F32)


def _mx(x):
    return x.astype(MXU_DTYPE)


def _rms_norm(x, g):
    ms = jnp.mean(x * x, axis=-1, keepdims=True)
    return x * lax.rsqrt(ms + RMS_EPS) * g


def _pair_split_rows(x, lo_mask):
    return jnp.concatenate([jnp.where(lo_mask, x, 0.0), jnp.where(lo_mask, 0.0, x)], axis=0)


def _mix_kernel(x_ref, ng_ref, w_ssm_ref, w_small_ref, w_ret_ref, w_gla_ref, w_gate_ref,
                cw_ref, cb_ref, dtb_ref, alog_ref, dfull_ref, ssm_ng_ref,
                cos_ref, sin_ref, ret_d_ref, ret_qdec_ref, ret_wst_ref, ret_cdec_ref, ret_ng_ref,
                wa2_ref, ba_ref, gla_ng_ref, wb_ref, bg_ref, wout_ref,
                e2_ref, t2_ref, t64_ref,
                o_ref,
                h_s, xbc_buf, xact, zs, dt_s, la_s, retp, glap, lag,
                yssm, yret, ygla, ssd_state, ret_state, gla_state, *, tm):
    t = pl.program_id(1)

    @pl.when(t == 0)
    def _():
        xbc_buf[0:SUBLANES, :] = jnp.zeros((SUBLANES, SSM_CONV_DIM), F32)
        ssd_state[...] = jnp.zeros_like(ssd_state)
        ret_state[...] = jnp.zeros_like(ret_state)
        gla_state[...] = jnp.zeros_like(gla_state)

    lane = lax.broadcasted_iota(jnp.int32, (1, LANES), 1)
    lo_mask = lane < HALF
    rot_first = (lane % HALF) < (HALF // 2)

    h = _mx(_rms_norm(x_ref[...], ng_ref[...]))
    h_s[...] = h

    zx = _dot(h, w_ssm_ref[...])
    zs[...] = _silu(zx[:, :BRANCH_WIDTH])
    xbc_buf[SUBLANES:SUBLANES + tm, :] = zx[:, BRANCH_WIDTH:]
    conv = cb_ref[...]
    for k in range(SSM_CONV):
        off = SUBLANES - (SSM_CONV - 1) + k
        conv = conv + cw_ref[k:k + 1, :] * xbc_buf[off:off + tm, :]
    xbc_buf[0:SUBLANES, :] = xbc_buf[tm:tm + SUBLANES, :]
    xact[...] = _silu(conv)

    small = _dot(h, w_small_ref[...])
    dt = _softplus(small[:, :LANES] + dtb_ref[...])
    dt_s[...] = dt
    la_s[...] = dt * (-jnp.exp(alog_ref[...]))

    rp = _dot(h, w_ret_ref[...])
    cos = cos_ref[...]
    sin = sin_ref[...]
    for j in range(2 * BRANCH_WIDTH // LANES):
        xb = rp[:, j * LANES:(j + 1) * LANES]
        partner = jnp.where(rot_first, pltpu.roll(xb, LANES - HALF // 2, 1), pltpu.roll(xb, HALF // 2, 1))
        rot = xb * cos + partner * sin
        if j >= BRANCH_WIDTH // LANES:
            rot = rot * (RET_HEAD_DIM ** -0.5)
        retp[:, j * LANES:(j + 1) * LANES] = rot
    retp[:, 2 * BRANCH_WIDTH:3 * BRANCH_WIDTH] = rp[:, 2 * BRANCH_WIDTH:3 * BRANCH_WIDTH]
    retp[:, 3 * BRANCH_WIDTH:] = _silu(rp[:, 3 * BRANCH_WIDTH:])

    gp = _dot(h, w_gla_ref[...])
    glap[:, 0:2 * GLA_KEY + BRANCH_WIDTH] = gp[:, 0:2 * GLA_KEY + BRANCH_WIDTH]
    glap[:, 2 * GLA_KEY + BRANCH_WIDTH:] = _silu(gp[:, 2 * GLA_KEY + BRANCH_WIDTH:])
    al = _dot(_mx(small[:, LANES:]), wa2_ref[...]) + ba_ref[...]
    lag[...] = (jnp.minimum(al, 0.0) - jnp.log(1.0 + jnp.exp(-jnp.abs(al)))) * (1.0 / GLA_GATE_NORMALIZER)

    row = lax.broadcasted_iota(jnp.int32, (CHUNK, CHUNK), 0)
    col = lax.broadcasted_iota(jnp.int32, (CHUNK, CHUNK), 1)
    causal = row >= col
    pair_diag = (row // HALF) == (col // HALF)
    srow = lax.broadcasted_iota(jnp.int32, (CHUNK, BRANCH_WIDTH), 0)
    scol = lax.broadcasted_iota(jnp.int32, (CHUNK, BRANCH_WIDTH), 1)
    ssd_diag = (srow // HALF) == (scol // (BRANCH_WIDTH // 2))
    grow = lax.broadcasted_iota(jnp.int32, (2 * GLA_CHUNK, GLA_CHUNK), 0)
    gcol = lax.broadcasted_iota(jnp.int32, (2 * GLA_CHUNK, GLA_CHUNK), 1)
    gla_causal = (grow % GLA_CHUNK) >= gcol

    def chunk_body(c, carry):
        r0 = pl.multiple_of(c * CHUNK, CHUNK)
        rows = pl.ds(r0, CHUNK)

        cum = _const_dot(t2_ref[...], la_s[rows, :])
        cum_t = cum.T
        ecum = jnp.exp(cum)
        wst = jnp.exp(cum[CHUNK - 1:CHUNK, :] - cum)
        full = _dot_const(jnp.concatenate([dt_s[rows, :], ecum, wst], axis=0), e2_ref[...])
        dt_full = full[0:CHUNK]
        ecum_full = full[CHUNK:2 * CHUNK]
        wst_full = full[2 * CHUNK:3 * CHUNK]
        xs = xact[rows, 0:BRANCH_WIDTH]
        bm = _mx(xact[rows, BRANCH_WIDTH:BRANCH_WIDTH + LANES])
        cm = xact[rows, BRANCH_WIDTH + LANES:BRANCH_WIDTH + 2 * LANES]
        v = xs * dt_full
        qk = _dot_nt(_mx(_pair_split_rows(cm, lo_mask)), bm)
        s_prev = ssd_state[...]
        y_inter = _dot(_mx(cm), _mx(s_prev)) * ecum_full
        ys = []
        for j in range(SSM_HEADS // 2):
            g = j // 2
            qkg = qk[g * CHUNK:(g + 1) * CHUNK, :]
            sc = []
            for hh in (2 * j, 2 * j + 1):
                seg = cum[:, hh:hh + 1] - cum_t[hh:hh + 1, :]
                dm = jnp.exp(jnp.where(causal, seg, NEG_BIG))
                sc.append(_mx(qkg * dm))
            vj = v[:, j * LANES:(j + 1) * LANES]
            ys.append(_dot(jnp.concatenate(sc, axis=1), _mx(_pair_split_rows(vj, lo_mask))))
        yssm[rows, :] = jnp.concatenate(ys, axis=1) + y_inter
        kv = _dot(_mx(xact[rows, BRANCH_WIDTH:BRANCH_WIDTH + LANES].T), _mx(v * wst_full))
        ssd_state[...] = s_prev * ecum_full[CHUNK - 1:CHUNK, :] + jnp.where(ssd_diag, kv, 0.0)

        q = retp[rows, 0:BRANCH_WIDTH]
        k = retp[rows, BRANCH_WIDTH:2 * BRANCH_WIDTH]
        vr = retp[rows, 2 * BRANCH_WIDTH:3 * BRANCH_WIDTH]
        qd = _mx(q * ret_qdec_ref[...])
        ks = k * ret_wst_ref[...]
        for j in range(RET_HEADS // 2):
            blk = slice(j * LANES, (j + 1) * LANES)
            vj = vr[:, blk]
            qkp = _dot_nt(_mx(_pair_split_rows(q[:, blk], lo_mask)), _mx(k[:, blk]))
            sc = _mx(qkp * ret_d_ref[j])
            sj = ret_state[j]
            lhs = jnp.concatenate([sc[0:CHUNK], sc[CHUNK:2 * CHUNK], qd[:, blk]], axis=1)
            rhs = jnp.concatenate([_mx(_pair_split_rows(vj, lo_mask)), _mx(sj)], axis=0)
            yret[rows, blk] = _dot(lhs, rhs)
            kvj = _dot(_mx(ks[:, blk].T), _mx(vj))
            ret_state[j] = sj * ret_cdec_ref[:, blk] + jnp.where(pair_diag, kvj, 0.0)

        for cc in range(CHUNK // GLA_CHUNK):
            rr = pl.ds(r0 + cc * GLA_CHUNK, GLA_CHUNK)
            cumg = _const_dot(t64_ref[...], lag[rr, :])
            last = cumg[GLA_CHUNK - 1:GLA_CHUNK, :]
            gk = glap[rr, GLA_KEY:2 * GLA_KEY]
            qdec = glap[rr, 0:GLA_KEY] * (HALF ** -0.5) * jnp.exp(cumg)
            kinv = gk * jnp.exp(-cumg)
            kst = gk * jnp.exp(last - cumg)
            elast = jnp.exp(last)
            for j in range(GLA_HEADS // 2):
                blk = slice(j * LANES, (j + 1) * LANES)
                qj = qdec[:, blk]
                sc = _dot_nt(_mx(_pair_split_rows(qj, lo_mask)), _mx(kinv[:, blk]))
                sc = _mx(jnp.where(gla_causal, sc, 0.0))
                for hh in range(2):
                    hd = 2 * j + hh
                    vblk = slice(2 * GLA_KEY + hd * LANES, 2 * GLA_KEY + (hd + 1) * LANES)
                    vh = glap[rr, vblk]
                    st = gla_state[hd]
                    y = _dot(sc[hh * GLA_CHUNK:(hh + 1) * GLA_CHUNK], _mx(vh)) + _dot_nt(_mx(qj), _mx(st))
                    ygla[rr, hd * LANES:(hd + 1) * LANES] = y
                    kvh = _dot(_mx(vh.T), _mx(kst[:, blk]))
                    head_mask = lo_mask if hh == 0 else jnp.logical_not(lo_mask)
                    gla_state[hd] = st * elast[:, blk] + jnp.where(head_mask, kvh, 0.0)
        return carry

    lax.fori_loop(0, tm // CHUNK, chunk_body, 0)

    y = (yssm[...] + dfull_ref[...] * xact[:, 0:BRANCH_WIDTH]) * zs[...]
    parts = []
    gw = BRANCH_WIDTH // 2
    for g in range(2):
        yg = y[:, g * gw:(g + 1) * gw]
        parts.append(yg * lax.rsqrt(jnp.mean(yg * yg, axis=-1, keepdims=True) + GN_EPS))
    y_ssm = jnp.concatenate(parts, axis=1) * ssm_ng_ref[...]

    parts = []
    for j in range(RET_HEADS // 2):
        yb = yret[:, j * LANES:(j + 1) * LANES]
        s_all = jnp.sum(yb, axis=-1, keepdims=True)
        s_lo = jnp.sum(jnp.where(lo_mask, yb, 0.0), axis=-1, keepdims=True)
        yc = yb - jnp.where(lo_mask, s_lo, s_all - s_lo) * (1.0 / HALF)
        sq = yc * yc
        v_all = jnp.sum(sq, axis=-1, keepdims=True)
        v_lo = jnp.sum(jnp.where(lo_mask, sq, 0.0), axis=-1, keepdims=True)
        var = jnp.where(lo_mask, v_lo, v_all - v_lo) * (1.0 / HALF)
        parts.append(yc * lax.rsqrt(var + GN_EPS))
    y_ret = jnp.concatenate(parts, axis=1) * ret_ng_ref[...] * retp[:, 3 * BRANCH_WIDTH:]

    parts = []
    for hd in range(GLA_HEADS):
        yb = ygla[:, hd * LANES:(hd + 1) * LANES]
        parts.append(yb * lax.rsqrt(jnp.mean(yb * yb, axis=-1, keepdims=True) + GN_EPS))
    y_gla = jnp.concatenate(parts, axis=1) * gla_ng_ref[...] * glap[:, 2 * GLA_KEY + BRANCH_WIDTH:]

    h = h_s[...]
    merged = None
    for i, yb in enumerate((y_ssm, y_ret, y_gla)):
        cols = slice(i * D_MODEL, (i + 1) * D_MODEL)
        gate = _sigmoid(_dot(h, w_gate_ref[:, cols]) + bg_ref[:, cols])
        term = gate * _dot(_mx(yb), wb_ref[i])
        merged = term if merged is None else merged + term
    o_ref[...] = x_ref[...] + _dot(_mx(merged), wout_ref[...])


def _ffn_kernel(x_ref, ng_ref, wup_ref, cw_ref, cb_ref, wdown_ref, fg_ref, o_ref, u_buf, a_s, *, tm, final_norm):
    t = pl.program_id(1)

    @pl.when(t == 0)
    def _():
        u_buf[0:SUBLANES, :] = jnp.zeros((SUBLANES, 2 * D_FF), F32)

    x = x_ref[...]
    h = _mx(_rms_norm(x, ng_ref[...]))
    for j in range(D_FF // FFN_BW):
        acts = []
        for part in range(2):
            cols = slice(part * D_FF + j * FFN_BW, part * D_FF + (j + 1) * FFN_BW)
            u_buf[SUBLANES:SUBLANES + tm, cols] = _dot(h, wup_ref[:, cols])
            conv = cb_ref[:, cols]
            for k in range(FFN_CONV):
                off = SUBLANES - (FFN_CONV - 1) + k
                conv = conv + cw_ref[k:k + 1, cols] * u_buf[off:off + tm, cols]
            acts.append(conv)
        a_s[:, j * FFN_BW:(j + 1) * FFN_BW] = _mx(_silu(acts[0]) * acts[1])
    u_buf[0:SUBLANES, :] = u_buf[tm:tm + SUBLANES, :]
    y = x + _dot(a_s[...], wdown_ref[...])
    if final_norm:
        y = _rms_norm(y, fg_ref[...])
    o_ref[...] = y


def _const_spec(shape):
    nd = len(shape)
    return pl.BlockSpec(shape, lambda b, t, _nd=nd: (0,) * _nd, pipeline_mode=pl.Buffered(1))


def _mixer_constants(seq):
    half = RET_HEAD_DIM // 2
    inv_freq = ROPE_BASE ** (-np.arange(half, dtype=np.float64) / half)
    ang = np.arange(seq, dtype=np.float64)[:, None] * inv_freq[None, :]
    lane = np.arange(LANES)
    cos = np.cos(ang)[:, lane % half]
    sin = np.sin(ang)[:, lane % half] * np.where((lane % RET_HEAD_DIM) < half, -1.0, 1.0)[None, :]

    log_gamma = np.log(1.0 - np.exp2(-5.0 - np.arange(RET_HEADS, dtype=np.float64)))
    pos = np.arange(CHUNK, dtype=np.float64)
    diff = pos[:, None] - pos[None, :]
    dmat = np.where(diff >= 0, np.exp(np.maximum(diff, 0.0)[None] * log_gamma[:, None, None]), 0.0)
    ret_d = dmat.reshape(RET_HEADS // 2, 2 * CHUNK, CHUNK)
    lg_full = np.repeat(log_gamma, RET_HEAD_DIM)
    ret_qdec = np.exp((pos[:, None] + 1.0) * lg_full[None, :])
    ret_wst = np.exp((CHUNK - 1.0 - pos[:, None]) * lg_full[None, :])
    ret_cdec = np.exp(CHUNK * lg_full)[None, :]

    expand = (np.arange(LANES)[:, None] == (np.arange(BRANCH_WIDTH)[None, :] // HALF)).astype(np.float64)
    e2 = np.concatenate([expand, expand], axis=0)
    tri = (pos[:, None] >= pos[None, :]).astype(np.float64)
    t2 = np.concatenate([tri, tri], axis=1)
    tri64 = tri[:GLA_CHUNK, :GLA_CHUNK]
    t64 = np.concatenate([tri64, tri64], axis=1)
    f = lambda a: jnp.asarray(a, F32)
    m = lambda a: jnp.asarray(a, MXU_DTYPE)
    return dict(cos=f(cos), sin=f(sin), ret_d=f(ret_d), ret_qdec=f(ret_qdec), ret_wst=f(ret_wst),
                ret_cdec=f(ret_cdec), e2=m(e2), t2=m(t2), t64=m(t64))


def _pad_lanes(v, width=LANES):
    v = v.reshape(1, -1).astype(F32)
    return jnp.pad(v, ((0, 0), (0, width - v.shape[1])))


def _mix_layer(x, consts, ng, w_in, conv_w, conv_b, dt_bias, a_log, ssm_d, ssm_ng, ret_ng,
               wa2, ba, gla_ng, w_branch, b_gate, w_out):
    bsz, seq, d = x.shape
    tm = MIX_TM
    assert seq % tm == 0 and tm % CHUNK == 0 and d == D_MODEL

    w_in = w_in.astype(MXU_DTYPE)
    w_ssm = w_in[:, _OFF_Z:_OFF_DT]
    w_small = jnp.zeros((d, 2 * LANES), MXU_DTYPE)
    w_small = w_small.at[:, 0:SSM_HEADS].set(w_in[:, _OFF_DT:_OFF_RET])
    w_small = w_small.at[:, LANES:LANES + GLA_RANK].set(w_in[:, _OFF_GLR:_OFF_GATE])
    w_ret = w_in[:, _OFF_RET:_OFF_GLA]
    w_gla = w_in[:, _OFF_GLA:_OFF_GLR]
    w_gate = w_in[:, _OFF_GATE:]
    wa2_p = jnp.zeros((LANES, GLA_KEY), MXU_DTYPE).at[0:GLA_RANK].set(wa2.astype(MXU_DTYPE))

    row = lambda v: v.reshape(1, -1).astype(F32)
    operands = [
        (x, pl.BlockSpec((None, tm, d), lambda b, t: (b, t, 0))),
        (row(ng), None), (w_ssm, None), (w_small, None), (w_ret, None), (w_gla, None), (w_gate, None),
        (conv_w.astype(F32), None), (row(conv_b), None), (_pad_lanes(dt_bias), None), (_pad_lanes(a_log), None),
        (row(jnp.repeat(ssm_d, HALF)), None), (row(ssm_ng), None),
        (consts["cos"], pl.BlockSpec((tm, LANES), lambda b, t: (t, 0))),
        (consts["sin"], pl.BlockSpec((tm, LANES), lambda b, t: (t, 0))),
        (consts["ret_d"], None), (consts["ret_qdec"], None), (consts["ret_wst"], None), (consts["ret_cdec"], None),
        (row(ret_ng), None), (wa2_p, None), (row(ba), None), (row(gla_ng), None),
        (w_branch.astype(MXU_DTYPE), None), (row(b_gate), None), (w_out.astype(MXU_DTYPE), None),
        (consts["e2"], None), (consts["t2"], None), (consts["t64"], None),
    ]
    args = [a for a, _ in operands]
    specs = [s if s is not None else _const_spec(a.shape) for a, s in operands]
    scratch = [
        pltpu.VMEM((tm, d), MXU_DTYPE),
        pltpu.VMEM((SUBLANES + tm, SSM_CONV_DIM), F32),
        pltpu.VMEM((tm, SSM_CONV_DIM), F32),
        pltpu.VMEM((tm, BRANCH_WIDTH), F32),
        pltpu.VMEM((tm, LANES), F32),
        pltpu.VMEM((tm, LANES), F32),
        pltpu.VMEM((tm, 4 * BRANCH_WIDTH), F32),
        pltpu.VMEM((tm, 2 * GLA_KEY + 2 * BRANCH_WIDTH), F32),
        pltpu.VMEM((tm, GLA_KEY), F32),
        pltpu.VMEM((tm, BRANCH_WIDTH), F32),
        pltpu.VMEM((tm, BRANCH_WIDTH), F32),
        pltpu.VMEM((tm, BRANCH_WIDTH), F32),
        pltpu.VMEM((CHUNK, BRANCH_WIDTH), F32),
        pltpu.VMEM((RET_HEADS // 2, LANES, LANES), F32),
        pltpu.VMEM((GLA_HEADS, LANES, LANES), F32),
    ]
    return pl.pallas_call(
        functools.partial(_mix_kernel, tm=tm),
        grid=(bsz, seq // tm),
        in_specs=specs,
        out_specs=pl.BlockSpec((None, tm, d), lambda b, t: (b, t, 0)),
        out_shape=jax.ShapeDtypeStruct(x.shape, x.dtype),
        scratch_shapes=scratch,
        compiler_params=pltpu.CompilerParams(dimension_semantics=("arbitrary", "arbitrary"),
                                             vmem_limit_bytes=VMEM_LIMIT_BYTES),
        name="mixer_layer",
    )(*args)


def _ffn_layer(x, ng, w_up, conv_w, conv_b, w_down, final_g, final_norm):
    bsz, seq, d = x.shape
    tm = FFN_TM
    assert seq % tm == 0 and d == D_MODEL and D_FF % FFN_BW == 0
    row = lambda v: v.reshape(1, -1).astype(F32)
    args = [x, row(ng), w_up.astype(MXU_DTYPE), conv_w.astype(F32), row(conv_b), w_down.astype(MXU_DTYPE), row(final_g)]
    specs = [pl.BlockSpec((None, tm, d), lambda b, t: (b, t, 0))] + [_const_spec(a.shape) for a in args[1:]]
    return pl.pallas_call(
        functools.partial(_ffn_kernel, tm=tm, final_norm=final_norm),
        grid=(bsz, seq // tm),
        in_specs=specs,
        out_specs=pl.BlockSpec((None, tm, d), lambda b, t: (b, t, 0)),
        out_shape=jax.ShapeDtypeStruct(x.shape, x.dtype),
        scratch_shapes=[pltpu.VMEM((SUBLANES + tm, 2 * D_FF), F32), pltpu.VMEM((tm, D_FF), MXU_DTYPE)],
        compiler_params=pltpu.CompilerParams(dimension_semantics=("arbitrary", "arbitrary"),
                                             vmem_limit_bytes=VMEM_LIMIT_BYTES),
        name="ffn_layer",
    )(*args)


def kernel(x, norm_mix_g, w_in, ssm_conv_w, ssm_conv_b, ssm_dt_bias, ssm_a_log, ssm_d, ssm_norm_g, ret_norm_g, gla_w_alpha2, gla_b_alpha, gla_norm_g, w_branch, b_gate, w_out, norm_ffn_g, w_up, ffn_conv_w, ffn_conv_b, w_down, norm_f_g):
    depth = w_in.shape[0]
    consts = _mixer_constants(x.shape[1])
    for i in range(depth):
        x = _mix_layer(x, consts, norm_mix_g[i], w_in[i], ssm_conv_w[i], ssm_conv_b[i], ssm_dt_bias[i],
                       ssm_a_log[i], ssm_d[i], ssm_norm_g[i], ret_norm_g[i], gla_w_alpha2[i], gla_b_alpha[i],
                       gla_norm_g[i], w_branch[i], b_gate[i], w_out[i])
        x = _ffn_layer(x, norm_ffn_g[i], w_up[i], ffn_conv_w[i], ffn_conv_b[i], w_down[i], norm_f_g,
                       final_norm=(i == depth - 1))
    return x
```

```python
import functools
import math

import numpy as np
import jax
import jax.numpy as jnp
from jax import lax
from jax.experimental import pallas as pl
from jax.experimental.pallas import tpu as pltpu

F32 = jnp.float32
MXU_DTYPE = jnp.bfloat16

D_MODEL = 1024
BRANCH_WIDTH = 512
N_BRANCH = 3
CHUNK = 128
GLA_CHUNK = 64
SSM_HEADS = 8
SSM_CONV = 4
SSM_CONV_DIM = 768
RET_HEADS = 8
RET_HEAD_DIM = 64
ROPE_BASE = 10000.0
GLA_HEADS = 4
GLA_KEY = 256
GLA_RANK = 16
GLA_GATE_NORMALIZER = 16.0
D_FF = 2816
FFN_CONV = 3
RMS_EPS = 1e-6
GN_EPS = 1e-5

_OFF_Z = 0
_OFF_XBC = 512
_OFF_DT = 1280
_OFF_RET = 1288
_OFF_GLA = 3336
_OFF_GLR = 4872
_OFF_GATE = 4888
N_IN = 7960

LANES = 128
SUBLANES = 8
HALF = 64
MIX_TM = 256
FFN_TM = 512
FFN_BW = 256
VMEM_LIMIT_BYTES = 60000 * 1024
NEG_BIG = -1e30


def _sigmoid(x):
    return 1.0 / (1.0 + jnp.exp(-x))


def _silu(x):
    return x * _sigmoid(x)


def _softplus(x):
    return jnp.maximum(x, 0.0) + jnp.log(1.0 + jnp.exp(-jnp.abs(x)))


def _split2(x):
    hi = x.astype(MXU_DTYPE)
    lo = (x - hi.astype(F32)).astype(MXU_DTYPE)
    return hi, lo


def _const_dot(c2, x):
    hi, lo = _split2(x)
    return jnp.dot(c2, jnp.concatenate([hi, lo], axis=0), preferred_element_type=F32)


def _dot_const(x, c2):
    hi, lo = _split2(x)
    return jnp.dot(jnp.concatenate([hi, lo], axis=1), c2, preferred_element_type=F32)


def _dot(a, b):
    return jnp.dot(a, b, preferred_element_type=F32)


def _dot_nt(a, b):
    return lax.dot_general(a, b, (((1,), (1,)), ((), ())), preferred_element_type=F32)


def _mx(x):
    return x.astype(MXU_DTYPE)


def _rms_norm(x, g):
    ms = jnp.mean(x * x, axis=-1, keepdims=True)
    return x * lax.rsqrt(ms + RMS_EPS) * g


def _pair_split_rows(x, lo_mask):
    return jnp.concatenate([jnp.where(lo_mask, x, 0.0), jnp.where(lo_mask, 0.0, x)], axis=0)


def _mix_kernel(x_ref, ng_ref, w_ssm_ref, w_small_ref, w_ret_ref, w_gla_ref, w_gate_ref,
                cw_ref, cb_ref, dtb_ref, alog_ref, dfull_ref, ssm_ng_ref,
                cos_ref, sin_ref, ret_d_ref, ret_qdec_ref, ret_wst_ref, ret_cdec_ref, ret_ng_ref,
                wa2_ref, ba_ref, gla_ng_ref, wb_ref, bg_ref, wout_ref,
                e2_ref, t2_ref, t64_ref,
                o_ref,
                h_s, xbc_buf, xact, zs, dt_s, la_s, retp, glap, lag,
                yssm, yret, ygla, ssd_state, ret_state, gla_state, *, tm):
    t = pl.program_id(1)

    @pl.when(t == 0)
    def _():
        xbc_buf[0:SUBLANES, :] = jnp.zeros((SUBLANES, SSM_CONV_DIM), F32)
        ssd_state[...] = jnp.zeros_like(ssd_state)
        ret_state[...] = jnp.zeros_like(ret_state)
        gla_state[...] = jnp.zeros_like(gla_state)

    lane = lax.broadcasted_iota(jnp.int32, (1, LANES), 1)
    lo_mask = lane < HALF
    rot_first = (lane % HALF) < (HALF // 2)

    h = _mx(_rms_norm(x_ref[...], ng_ref[...]))
    h_s[...] = h

    zx = _dot(h, w_ssm_ref[...])
    zs[...] = _silu(zx[:, :BRANCH_WIDTH])
    xbc_buf[SUBLANES:SUBLANES + tm, :] = zx[:, BRANCH_WIDTH:]
    conv = cb_ref[...]
    for k in range(SSM_CONV):
        off = SUBLANES - (SSM_CONV - 1) + k
        conv = conv + cw_ref[k:k + 1, :] * xbc_buf[off:off + tm, :]
    xbc_buf[0:SUBLANES, :] = xbc_buf[tm:tm + SUBLANES, :]
    xact[...] = _silu(conv)

    small = _dot(h, w_small_ref[...])
    dt = _softplus(small[:, :LANES] + dtb_ref[...])
    dt_s[...] = dt
    la_s[...] = dt * (-jnp.exp(alog_ref[...]))

    rp = _dot(h, w_ret_ref[...])
    cos = cos_ref[...]
    sin = sin_ref[...]
    for j in range(2 * BRANCH_WIDTH // LANES):
        xb = rp[:, j * LANES:(j + 1) * LANES]
        partner = jnp.where(rot_first, pltpu.roll(xb, LANES - HALF // 2, 1), pltpu.roll(xb, HALF // 2, 1))
        rot = xb * cos + partner * sin
        if j >= BRANCH_WIDTH // LANES:
            rot = rot * (RET_HEAD_DIM ** -0.5)
        retp[:, j * LANES:(j + 1) * LANES] = rot
    retp[:, 2 * BRANCH_WIDTH:3 * BRANCH_WIDTH] = rp[:, 2 * BRANCH_WIDTH:3 * BRANCH_WIDTH]
    retp[:, 3 * BRANCH_WIDTH:] = _silu(rp[:, 3 * BRANCH_WIDTH:])

    gp = _dot(h, w_gla_ref[...])
    glap[:, 0:2 * GLA_KEY + BRANCH_WIDTH] = gp[:, 0:2 * GLA_KEY + BRANCH_WIDTH]
    glap[:, 2 * GLA_KEY + BRANCH_WIDTH:] = _silu(gp[:, 2 * GLA_KEY + BRANCH_WIDTH:])
    al = _dot(_mx(small[:, LANES:]), wa2_ref[...]) + ba_ref[...]
    lag[...] = (jnp.minimum(al, 0.0) - jnp.log(1.0 + jnp.exp(-jnp.abs(al)))) * (1.0 / GLA_GATE_NORMALIZER)

    row = lax.broadcasted_iota(jnp.int32, (CHUNK, CHUNK), 0)
    col = lax.broadcasted_iota(jnp.int32, (CHUNK, CHUNK), 1)
    causal = row >= col
    pair_diag = (row // HALF) == (col // HALF)
    srow = lax.broadcasted_iota(jnp.int32, (CHUNK, BRANCH_WIDTH), 0)
    scol = lax.broadcasted_iota(jnp.int32, (CHUNK, BRANCH_WIDTH), 1)
    ssd_diag = (srow // HALF) == (scol // (BRANCH_WIDTH // 2))
    grow = lax.broadcasted_iota(jnp.int32, (2 * GLA_CHUNK, GLA_CHUNK), 0)
    gcol = lax.broadcasted_iota(jnp.int32, (2 * GLA_CHUNK, GLA_CHUNK), 1)
    gla_causal = (grow % GLA_CHUNK) >= gcol

    nch = tm // CHUNK
    ngc = tm // GLA_CHUNK
    n_pairs = BRANCH_WIDTH // LANES
    rws = [slice(c * CHUNK, (c + 1) * CHUNK) for c in range(nch)]
    grw = [slice(c * GLA_CHUNK, (c + 1) * GLA_CHUNK) for c in range(ngc)]
    blks = [slice(j * LANES, (j + 1) * LANES) for j in range(n_pairs)]
    col_b = slice(BRANCH_WIDTH, BRANCH_WIDTH + LANES)
    col_c = slice(BRANCH_WIDTH + LANES, BRANCH_WIDTH + 2 * LANES)


    cum = [_const_dot(t2_ref[...], la_s[r, :]) for r in rws]
    cumg = [_const_dot(t64_ref[...], lag[r, :]) for r in grw]

    ssd_full = []
    for c, r in enumerate(rws):
        ecum = jnp.exp(cum[c])
        wst = jnp.exp(cum[c][CHUNK - 1:CHUNK, :] - cum[c])
        ssd_full.append(_dot_const(jnp.concatenate([dt_s[r, :], ecum, wst], axis=0), e2_ref[...]))
    gla_q, gla_kinv, gla_kst, gla_elast = [], [], [], []
    for c, r in enumerate(grw):
        last = cumg[c][GLA_CHUNK - 1:GLA_CHUNK, :]
        gk = glap[r, GLA_KEY:2 * GLA_KEY]
        gla_q.append(glap[r, 0:GLA_KEY] * (HALF ** -0.5) * jnp.exp(cumg[c]))
        gla_kinv.append(_mx(gk * jnp.exp(-cumg[c])))
        gla_kst.append(_mx(gk * jnp.exp(last - cumg[c])))
        gla_elast.append(jnp.exp(last))

    ssd_v, ssd_qk, ssd_kv = [], [], []
    for c, r in enumerate(rws):
        v = xact[r, 0:BRANCH_WIDTH] * ssd_full[c][0:CHUNK]
        ssd_v.append(v)
        bm = xact[r, col_b]
        ssd_qk.append(_dot_nt(_mx(_pair_split_rows(xact[r, col_c], lo_mask)), _mx(bm)))
        kv = _dot(_mx(bm.T), _mx(v * ssd_full[c][2 * CHUNK:3 * CHUNK]))
        ssd_kv.append(jnp.where(ssd_diag, kv, 0.0))
    ret_qk, ret_kv = [], []
    for c, r in enumerate(rws):
        ks = retp[r, BRANCH_WIDTH:2 * BRANCH_WIDTH] * ret_wst_ref[...]
        for j, blk in enumerate(blks):
            q = retp[r, blk]
            k = retp[r, BRANCH_WIDTH + j * LANES:BRANCH_WIDTH + (j + 1) * LANES]
            vj = retp[r, 2 * BRANCH_WIDTH + j * LANES:2 * BRANCH_WIDTH + (j + 1) * LANES]
            ret_qk.append(_dot_nt(_mx(_pair_split_rows(q, lo_mask)), _mx(k)))
            ret_kv.append(jnp.where(pair_diag, _dot(_mx(ks[:, blk].T), _mx(vj)), 0.0))
    gla_sc, gla_kv = [], []
    for c, r in enumerate(grw):
        for j in range(GLA_HEADS // 2):
            blk = blks[j]
            sc = _dot_nt(_mx(_pair_split_rows(gla_q[c][:, blk], lo_mask)), gla_kinv[c][:, blk])
            gla_sc.append(_mx(jnp.where(gla_causal, sc, 0.0)))
            for hh in range(2):
                hd = 2 * j + hh
                vh = glap[r, 2 * GLA_KEY + hd * LANES:2 * GLA_KEY + (hd + 1) * LANES]
                kvh = _dot(_mx(vh.T), gla_kst[c][:, blk])
                gla_kv.append(jnp.where(lo_mask if hh == 0 else jnp.logical_not(lo_mask), kvh, 0.0))

    ssd_intra = []
    for c, r in enumerate(rws):
        cum_t = cum[c].T
        ys = []
        for j in range(SSM_HEADS // 2):
            g = j // 2
            qkg = ssd_qk[c][g * CHUNK:(g + 1) * CHUNK, :]
            sc = []
            for hh in (2 * j, 2 * j + 1):
                seg = cum[c][:, hh:hh + 1] - cum_t[hh:hh + 1, :]
                sc.append(_mx(qkg * jnp.exp(jnp.where(causal, seg, NEG_BIG))))
            vj = ssd_v[c][:, blks[j]]
            ys.append(_dot(jnp.concatenate(sc, axis=1), _mx(_pair_split_rows(vj, lo_mask))))
        ssd_intra.append(jnp.concatenate(ys, axis=1))

    s_cur = ssd_state[...]
    for c, r in enumerate(rws):
        ecum_full = ssd_full[c][CHUNK:2 * CHUNK]
        yssm[r, :] = ssd_intra[c] + _dot(_mx(xact[r, col_c]), _mx(s_cur)) * ecum_full
        s_cur = s_cur * ecum_full[CHUNK - 1:CHUNK, :] + ssd_kv[c]
    ssd_state[...] = s_cur

    for j, blk in enumerate(blks):
        sj = ret_state[j]
        for c, r in enumerate(rws):
            sc = _mx(ret_qk[c * n_pairs + j] * ret_d_ref[j])
            qd = _mx(retp[r, blk] * ret_qdec_ref[:, blk])
            vj = retp[r, 2 * BRANCH_WIDTH + j * LANES:2 * BRANCH_WIDTH + (j + 1) * LANES]
            lhs = jnp.concatenate([sc[0:CHUNK], sc[CHUNK:2 * CHUNK], qd], axis=1)
            rhs = jnp.concatenate([_mx(_pair_split_rows(vj, lo_mask)), _mx(sj)], axis=0)
            yret[r, blk] = _dot(lhs, rhs)
            sj = sj * ret_cdec_ref[:, blk] + ret_kv[c * n_pairs + j]
        ret_state[j] = sj

    for hd in range(GLA_HEADS):
        j, hh = hd // 2, hd % 2
        blk = blks[j]
        st = gla_state[hd]
        for c, r in enumerate(grw):
            vh = glap[r, 2 * GLA_KEY + hd * LANES:2 * GLA_KEY + (hd + 1) * LANES]
            sc = gla_sc[c * (GLA_HEADS // 2) + j][hh * GLA_CHUNK:(hh + 1) * GLA_CHUNK]
            ygla[r, hd * LANES:(hd + 1) * LANES] = _dot(sc, _mx(vh)) + _dot_nt(_mx(gla_q[c][:, blk]), _mx(st))
            st = st * gla_elast[c][:, blk] + gla_kv[c * GLA_HEADS + hd]
        gla_state[hd] = st

    y = (yssm[...] + dfull_ref[...] * xact[:, 0:BRANCH_WIDTH]) * zs[...]
    parts = []
    gw = BRANCH_WIDTH // 2
    for g in range(2):
        yg = y[:, g * gw:(g + 1) * gw]
        parts.append(yg * lax.rsqrt(jnp.mean(yg * yg, axis=-1, keepdims=True) + GN_EPS))
    y_ssm = jnp.concatenate(parts, axis=1) * ssm_ng_ref[...]

    parts = []
    for j in range(RET_HEADS // 2):
        yb = yret[:, j * LANES:(j + 1) * LANES]
        s_all = jnp.sum(yb, axis=-1, keepdims=True)
        s_lo = jnp.sum(jnp.where(lo_mask, yb, 0.0), axis=-1, keepdims=True)
        yc = yb - jnp.where(lo_mask, s_lo, s_all - s_lo) * (1.0 / HALF)
        sq = yc * yc
        v_all = jnp.sum(sq, axis=-1, keepdims=True)
        v_lo = jnp.sum(jnp.where(lo_mask, sq, 0.0), axis=-1, keepdims=True)
        var = jnp.where(lo_mask, v_lo, v_all - v_lo) * (1.0 / HALF)
        parts.append(yc * lax.rsqrt(var + GN_EPS))
    y_ret = jnp.concatenate(parts, axis=1) * ret_ng_ref[...] * retp[:, 3 * BRANCH_WIDTH:]

    parts = []
    for hd in range(GLA_HEADS):
        yb = ygla[:, hd * LANES:(hd + 1) * LANES]
        parts.append(yb * lax.rsqrt(jnp.mean(yb * yb, axis=-1, keepdims=True) + GN_EPS))
    y_gla = jnp.concatenate(parts, axis=1) * gla_ng_ref[...] * glap[:, 2 * GLA_KEY + BRANCH_WIDTH:]

    h = h_s[...]
    merged = None
    for i, yb in enumerate((y_ssm, y_ret, y_gla)):
        cols = slice(i * D_MODEL, (i + 1) * D_MODEL)
        gate = _sigmoid(_dot(h, w_gate_ref[:, cols]) + bg_ref[:, cols])
        term = gate * _dot(_mx(yb), wb_ref[i])
        merged = term if merged is None else merged + term
    o_ref[...] = x_ref[...] + _dot(_mx(merged), wout_ref[...])


def _ffn_kernel(x_ref, ng_ref, wup_ref, cw_ref, cb_ref, wdown_ref, fg_ref, o_ref, u_buf, a_s, *, tm, final_norm):
    t = pl.program_id(1)

    @pl.when(t == 0)
    def _():
        u_buf[0:SUBLANES, :] = jnp.zeros((SUBLANES, 2 * D_FF), F32)

    x = x_ref[...]
    h = _mx(_rms_norm(x, ng_ref[...]))
    for j in range(D_FF // FFN_BW):
        acts = []
        for part in range(2):
            cols = slice(part * D_FF + j * FFN_BW, part * D_FF + (j + 1) * FFN_BW)
            u_buf[SUBLANES:SUBLANES + tm, cols] = _dot(h, wup_ref[:, cols])
            conv = cb_ref[:, cols]
            for k in range(FFN_CONV):
                off = SUBLANES - (FFN_CONV - 1) + k
                conv = conv + cw_ref[k:k + 1, cols] * u_buf[off:off + tm, cols]
            acts.append(conv)
        a_s[:, j * FFN_BW:(j + 1) * FFN_BW] = _mx(_silu(acts[0]) * acts[1])
    u_buf[0:SUBLANES, :] = u_buf[tm:tm + SUBLANES, :]
    y = x + _dot(a_s[...], wdown_ref[...])
    if final_norm:
        y = _rms_norm(y, fg_ref[...])
    o_ref[...] = y


def _const_spec(shape):
    nd = len(shape)
    return pl.BlockSpec(shape, lambda b, t, _nd=nd: (0,) * _nd, pipeline_mode=pl.Buffered(1))


def _mixer_constants(seq):
    half = RET_HEAD_DIM // 2
    inv_freq = ROPE_BASE ** (-np.arange(half, dtype=np.float64) / half)
    ang = np.arange(seq, dtype=np.float64)[:, None] * inv_freq[None, :]
    lane = np.arange(LANES)
    cos = np.cos(ang)[:, lane % half]
    sin = np.sin(ang)[:, lane % half] * np.where((lane % RET_HEAD_DIM) < half, -1.0, 1.0)[None, :]

    log_gamma = np.log(1.0 - np.exp2(-5.0 - np.arange(RET_HEADS, dtype=np.float64)))
    pos = np.arange(CHUNK, dtype=np.float64)
    diff = pos[:, None] - pos[None, :]
    dmat = np.where(diff >= 0, np.exp(np.maximum(diff, 0.0)[None] * log_gamma[:, None, None]), 0.0)
    ret_d = dmat.reshape(RET_HEADS // 2, 2 * CHUNK, CHUNK)
    lg_full = np.repeat(log_gamma, RET_HEAD_DIM)
    ret_qdec = np.exp((pos[:, None] + 1.0) * lg_full[None, :])
    ret_wst = np.exp((CHUNK - 1.0 - pos[:, None]) * lg_full[None, :])
    ret_cdec = np.exp(CHUNK * lg_full)[None, :]

    expand = (np.arange(LANES)[:, None] == (np.arange(BRANCH_WIDTH)[None, :] // HALF)).astype(np.float64)
    e2 = np.concatenate([expand, expand], axis=0)
    tri = (pos[:, None] >= pos[None, :]).astype(np.float64)
    t2 = np.concatenate([tri, tri], axis=1)
    tri64 = tri[:GLA_CHUNK, :GLA_CHUNK]
    t64 = np.concatenate([tri64, tri64], axis=1)
    f = lambda a: jnp.asarray(a, F32)
    m = lambda a: jnp.asarray(a, MXU_DTYPE)
    return dict(cos=f(cos), sin=f(sin), ret_d=f(ret_d), ret_qdec=f(ret_qdec), ret_wst=f(ret_wst),
                ret_cdec=f(ret_cdec), e2=m(e2), t2=m(t2), t64=m(t64))


def _pad_lanes(v, width=LANES):
    v = v.reshape(1, -1).astype(F32)
    return jnp.pad(v, ((0, 0), (0, width - v.shape[1])))


def _mix_layer(x, consts, ng, w_in, conv_w, conv_b, dt_bias, a_log, ssm_d, ssm_ng, ret_ng,
               wa2, ba, gla_ng, w_branch, b_gate, w_out):
    bsz, seq, d = x.shape
    tm = MIX_TM
    assert seq % tm == 0 and tm % CHUNK == 0 and d == D_MODEL

    w_in = w_in.astype(MXU_DTYPE)
    w_ssm = w_in[:, _OFF_Z:_OFF_DT]
    w_small = jnp.zeros((d, 2 * LANES), MXU_DTYPE)
    w_small = w_small.at[:, 0:SSM_HEADS].set(w_in[:, _OFF_DT:_OFF_RET])
    w_small = w_small.at[:, LANES:LANES + GLA_RANK].set(w_in[:, _OFF_GLR:_OFF_GATE])
    w_ret = w_in[:, _OFF_RET:_OFF_GLA]
    w_gla = w_in[:, _OFF_GLA:_OFF_GLR]
    w_gate = w_in[:, _OFF_GATE:]
    wa2_p = jnp.zeros((LANES, GLA_KEY), MXU_DTYPE).at[0:GLA_RANK].set(wa2.astype(MXU_DTYPE))

    row = lambda v: v.reshape(1, -1).astype(F32)
    operands = [
        (x, pl.BlockSpec((None, tm, d), lambda b, t: (b, t, 0))),
        (row(ng), None), (w_ssm, None), (w_small, None), (w_ret, None), (w_gla, None), (w_gate, None),
        (conv_w.astype(F32), None), (row(conv_b), None), (_pad_lanes(dt_bias), None), (_pad_lanes(a_log), None),
        (row(jnp.repeat(ssm_d, HALF)), None), (row(ssm_ng), None),
        (consts["cos"], pl.BlockSpec((tm, LANES), lambda b, t: (t, 0))),
        (consts["sin"], pl.BlockSpec((tm, LANES), lambda b, t: (t, 0))),
        (consts["ret_d"], None), (consts["ret_qdec"], None), (consts["ret_wst"], None), (consts["ret_cdec"], None),
        (row(ret_ng), None), (wa2_p, None), (row(ba), None), (row(gla_ng), None),
        (w_branch.astype(MXU_DTYPE), None), (row(b_gate), None), (w_out.astype(MXU_DTYPE), None),
        (consts["e2"], None), (consts["t2"], None), (consts["t64"], None),
    ]
    args = [a for a, _ in operands]
    specs = [s if s is not None else _const_spec(a.shape) for a, s in operands]
    scratch = [
        pltpu.VMEM((tm, d), MXU_DTYPE),
        pltpu.VMEM((SUBLANES + tm, SSM_CONV_DIM), F32),
        pltpu.VMEM((tm, SSM_CONV_DIM), F32),
        pltpu.VMEM((tm, BRANCH_WIDTH), F32),
        pltpu.VMEM((tm, LANES), F32),
        pltpu.VMEM((tm, LANES), F32),
        pltpu.VMEM((tm, 4 * BRANCH_WIDTH), F32),
        pltpu.VMEM((tm, 2 * GLA_KEY + 2 * BRANCH_WIDTH), F32),
        pltpu.VMEM((tm, GLA_KEY), F32),
        pltpu.VMEM((tm, BRANCH_WIDTH), F32),
        pltpu.VMEM((tm, BRANCH_WIDTH), F32),
        pltpu.VMEM((tm, BRANCH_WIDTH), F32),
        pltpu.VMEM((CHUNK, BRANCH_WIDTH), F32),
        pltpu.VMEM((RET_HEADS // 2, LANES, LANES), F32),
        pltpu.VMEM((GLA_HEADS, LANES, LANES), F32),
    ]
    return pl.pallas_call(
        functools.partial(_mix_kernel, tm=tm),
        grid=(bsz, seq // tm),
        in_specs=specs,
        out_specs=pl.BlockSpec((None, tm, d), lambda b, t: (b, t, 0)),
        out_shape=jax.ShapeDtypeStruct(x.shape, x.dtype),
        scratch_shapes=scratch,
        compiler_params=pltpu.CompilerParams(dimension_semantics=("arbitrary", "arbitrary"),
                                             vmem_limit_bytes=VMEM_LIMIT_BYTES),
        name="mixer_layer",
    )(*args)


def _ffn_layer(x, ng, w_up, conv_w, conv_b, w_down, final_g, final_norm):
    bsz, seq, d = x.shape
    tm = FFN_TM
    assert seq % tm == 0 and d == D_MODEL and D_FF % FFN_BW == 0
    row = lambda v: v.reshape(1, -1).astype(F32)
    args = [x, row(ng), w_up.astype(MXU_DTYPE), conv_w.astype(F32), row(conv_b), w_down.astype(MXU_DTYPE), row(final_g)]
    specs = [pl.BlockSpec((None, tm, d), lambda b, t: (b, t, 0))] + [_const_spec(a.shape) for a in args[1:]]
    return pl.pallas_call(
        functools.partial(_ffn_kernel, tm=tm, final_norm=final_norm),
        grid=(bsz, seq // tm),
        in_specs=specs,
        out_specs=pl.BlockSpec((None, tm, d), lambda b, t: (b, t, 0)),
        out_shape=jax.ShapeDtypeStruct(x.shape, x.dtype),
        scratch_shapes=[pltpu.VMEM((SUBLANES + tm, 2 * D_FF), F32), pltpu.VMEM((tm, D_FF), MXU_DTYPE)],
        compiler_params=pltpu.CompilerParams(dimension_semantics=("arbitrary", "arbitrary"),
                                             vmem_limit_bytes=VMEM_LIMIT_BYTES),
        name="ffn_layer",
    )(*args)


def kernel(x, norm_mix_g, w_in, ssm_conv_w, ssm_conv_b, ssm_dt_bias, ssm_a_log, ssm_d, ssm_norm_g, ret_norm_g, gla_w_alpha2, gla_b_alpha, gla_norm_g, w_branch, b_gate, w_out, norm_ffn_g, w_up, ffn_conv_w, ffn_conv_b, w_down, norm_f_g):
    depth = w_in.shape[0]
    consts = _mixer_constants(x.shape[1])
    for i in range(depth):
        x = _mix_layer(x, consts, norm_mix_g[i], w_in[i], ssm_conv_w[i], ssm_conv_b[i], ssm_dt_bias[i],
                       ssm_a_log[i], ssm_d[i], ssm_norm_g[i], ret_norm_g[i], gla_w_alpha2[i], gla_b_alpha[i],
                       gla_norm_g[i], w_branch[i], b_gate[i], w_out[i])
        x = _ffn_layer(x, norm_ffn_g[i], w_up[i], ffn_conv_w[i], ffn_conv_b[i], w_down[i], norm_f_g,
                       final_norm=(i == depth - 1))
    return x
```

```python
import functools
import math

import numpy as np
import jax
import jax.numpy as jnp
from jax import lax
from jax.experimental import pallas as pl
from jax.experimental.pallas import tpu as pltpu

F32 = jnp.float32
MXU_DTYPE = jnp.bfloat16

D_MODEL = 1024
BRANCH_WIDTH = 512
N_BRANCH = 3
CHUNK = 128
GLA_CHUNK = 64
SSM_HEADS = 8
SSM_CONV = 4
SSM_CONV_DIM = 768
RET_HEADS = 8
RET_HEAD_DIM = 64
ROPE_BASE = 10000.0
GLA_HEADS = 4
GLA_KEY = 256
GLA_RANK = 16
GLA_GATE_NORMALIZER = 16.0
D_FF = 2816
FFN_CONV = 3
RMS_EPS = 1e-6
GN_EPS = 1e-5

_OFF_Z = 0
_OFF_XBC = 512
_OFF_DT = 1280
_OFF_RET = 1288
_OFF_GLA = 3336
_OFF_GLR = 4872
_OFF_GATE = 4888
N_IN = 7960

LANES = 128
SUBLANES = 8
HALF = 64
MIX_TM = 512
FFN_TM = 512
FFN_BW = 256
VMEM_LIMIT_BYTES = 60000 * 1024
NEG_BIG = -1e30


def _sigmoid(x):
    return 1.0 / (1.0 + jnp.exp(-x))


def _silu(x):
    return x * _sigmoid(x)


def _softplus(x):
    return jnp.maximum(x, 0.0) + jnp.log(1.0 + jnp.exp(-jnp.abs(x)))


def _split2(x):
    hi = x.astype(MXU_DTYPE)
    lo = (x - hi.astype(F32)).astype(MXU_DTYPE)
    return hi, lo


def _const_dot(c2, x):
    hi, lo = _split2(x)
    return jnp.dot(c2, jnp.concatenate([hi, lo], axis=0), preferred_element_type=F32)


def _dot_const(x, c2):
    hi, lo = _split2(x)
    return jnp.dot(jnp.concatenate([hi, lo], axis=1), c2, preferred_element_type=F32)


def _dot(a, b):
    return jnp.dot(a, b, preferred_element_type=F32)


def _dot_nt(a, b):
    return lax.dot_general(a, b, (((1,), (1,)), ((), ())), preferred_element_type=F32)


def _mx(x):
    return x.astype(MXU_DTYPE)


def _rms_norm(x, g):
    ms = jnp.mean(x * x, axis=-1, keepdims=True)
    return x * lax.rsqrt(ms + RMS_EPS) * g


def _pair_split_rows(x, lo_mask):
    return jnp.concatenate([jnp.where(lo_mask, x, 0.0), jnp.where(lo_mask, 0.0, x)], axis=0)


def _mix_kernel(x_ref, ng_ref, w_ssm_ref, w_small_ref, w_ret_ref, w_gla_ref, w_gate_ref,
                cw_ref, cb_ref, dtb_ref, alog_ref, dfull_ref, ssm_ng_ref,
                cos_ref, sin_ref, ret_d_ref, ret_qdec_ref, ret_wst_ref, ret_cdec_ref, ret_ng_ref,
                wa2_ref, ba_ref, gla_ng_ref, wb_ref, bg_ref, wout_ref,
                e2_ref, t2_ref, t64_ref,
                o_ref,
                h_s, xbc_buf, xact, zs, dt_s, la_s, retp, glap, lag,
                yssm, yret, ygla, ssd_state, ret_state, gla_state, *, tm):
    t = pl.program_id(1)

    @pl.when(t == 0)
    def _():
        xbc_buf[0:SUBLANES, :] = jnp.zeros((SUBLANES, SSM_CONV_DIM), F32)
        ssd_state[...] = jnp.zeros_like(ssd_state)
        ret_state[...] = jnp.zeros_like(ret_state)
        gla_state[...] = jnp.zeros_like(gla_state)

    lane = lax.broadcasted_iota(jnp.int32, (1, LANES), 1)
    lo_mask = lane < HALF
    rot_first = (lane % HALF) < (HALF // 2)

    h = _mx(_rms_norm(x_ref[...], ng_ref[...]))
    h_s[...] = h

    zx = _dot(h, w_ssm_ref[...])
    zs[...] = _silu(zx[:, :BRANCH_WIDTH])
    xbc_buf[SUBLANES:SUBLANES + tm, :] = zx[:, BRANCH_WIDTH:]
    conv = cb_ref[...]
    for k in range(SSM_CONV):
        off = SUBLANES - (SSM_CONV - 1) + k
        conv = conv + cw_ref[k:k + 1, :] * xbc_buf[off:off + tm, :]
    xbc_buf[0:SUBLANES, :] = xbc_buf[tm:tm + SUBLANES, :]
    xact[...] = _silu(conv)

    small = _dot(h, w_small_ref[...])
    dt = _softplus(small[:, :LANES] + dtb_ref[...])
    dt_s[...] = dt
    la_s[...] = dt * (-jnp.exp(alog_ref[...]))

    rp = _dot(h, w_ret_ref[...])
    cos = cos_ref[...]
    sin = sin_ref[...]
    for j in range(2 * BRANCH_WIDTH // LANES):
        xb = rp[:, j * LANES:(j + 1) * LANES]
        partner = jnp.where(rot_first, pltpu.roll(xb, LANES - HALF // 2, 1), pltpu.roll(xb, HALF // 2, 1))
        rot = xb * cos + partner * sin
        if j >= BRANCH_WIDTH // LANES:
            rot = rot * (RET_HEAD_DIM ** -0.5)
        retp[:, j * LANES:(j + 1) * LANES] = rot
    retp[:, 2 * BRANCH_WIDTH:3 * BRANCH_WIDTH] = rp[:, 2 * BRANCH_WIDTH:3 * BRANCH_WIDTH]
    retp[:, 3 * BRANCH_WIDTH:] = _silu(rp[:, 3 * BRANCH_WIDTH:])

    gp = _dot(h, w_gla_ref[...])
    glap[:, 0:2 * GLA_KEY + BRANCH_WIDTH] = gp[:, 0:2 * GLA_KEY + BRANCH_WIDTH]
    glap[:, 2 * GLA_KEY + BRANCH_WIDTH:] = _silu(gp[:, 2 * GLA_KEY + BRANCH_WIDTH:])
    al = _dot(_mx(small[:, LANES:]), wa2_ref[...]) + ba_ref[...]
    lag[...] = (jnp.minimum(al, 0.0) - jnp.log(1.0 + jnp.exp(-jnp.abs(al)))) * (1.0 / GLA_GATE_NORMALIZER)

    row = lax.broadcasted_iota(jnp.int32, (CHUNK, CHUNK), 0)
    col = lax.broadcasted_iota(jnp.int32, (CHUNK, CHUNK), 1)
    causal = row >= col
    pair_diag = (row // HALF) == (col // HALF)
    srow = lax.broadcasted_iota(jnp.int32, (CHUNK, BRANCH_WIDTH), 0)
    scol = lax.broadcasted_iota(jnp.int32, (CHUNK, BRANCH_WIDTH), 1)
    ssd_diag = (srow // HALF) == (scol // (BRANCH_WIDTH // 2))
    grow = lax.broadcasted_iota(jnp.int32, (2 * GLA_CHUNK, GLA_CHUNK), 0)
    gcol = lax.broadcasted_iota(jnp.int32, (2 * GLA_CHUNK, GLA_CHUNK), 1)
    gla_causal = (grow % GLA_CHUNK) >= gcol

    nch = tm // CHUNK
    ngc = tm // GLA_CHUNK
    n_pairs = BRANCH_WIDTH // LANES
    rws = [slice(c * CHUNK, (c + 1) * CHUNK) for c in range(nch)]
    grw = [slice(c * GLA_CHUNK, (c + 1) * GLA_CHUNK) for c in range(ngc)]
    blks = [slice(j * LANES, (j + 1) * LANES) for j in range(n_pairs)]
    col_b = slice(BRANCH_WIDTH, BRANCH_WIDTH + LANES)
    col_c = slice(BRANCH_WIDTH + LANES, BRANCH_WIDTH + 2 * LANES)


    cum = [_const_dot(t2_ref[...], la_s[r, :]) for r in rws]
    cumg = [_const_dot(t64_ref[...], lag[r, :]) for r in grw]

    ssd_full = []
    for c, r in enumerate(rws):
        ecum = jnp.exp(cum[c])
        wst = jnp.exp(cum[c][CHUNK - 1:CHUNK, :] - cum[c])
        ssd_full.append(_dot_const(jnp.concatenate([dt_s[r, :], ecum, wst], axis=0), e2_ref[...]))
    gla_q, gla_kinv, gla_kst, gla_elast = [], [], [], []
    for c, r in enumerate(grw):
        last = cumg[c][GLA_CHUNK - 1:GLA_CHUNK, :]
        gk = glap[r, GLA_KEY:2 * GLA_KEY]
        gla_q.append(glap[r, 0:GLA_KEY] * (HALF ** -0.5) * jnp.exp(cumg[c]))
        gla_kinv.append(_mx(gk * jnp.exp(-cumg[c])))
        gla_kst.append(_mx(gk * jnp.exp(last - cumg[c])))
        gla_elast.append(jnp.exp(last))

    ssd_v, ssd_qk, ssd_kv = [], [], []
    for c, r in enumerate(rws):
        v = xact[r, 0:BRANCH_WIDTH] * ssd_full[c][0:CHUNK]
        ssd_v.append(v)
        bm = xact[r, col_b]
        ssd_qk.append(_dot_nt(_mx(_pair_split_rows(xact[r, col_c], lo_mask)), _mx(bm)))
        kv = _dot(_mx(bm.T), _mx(v * ssd_full[c][2 * CHUNK:3 * CHUNK]))
        ssd_kv.append(jnp.where(ssd_diag, kv, 0.0))
    ret_qk, ret_kv = [], []
    for c, r in enumerate(rws):
        ks = retp[r, BRANCH_WIDTH:2 * BRANCH_WIDTH] * ret_wst_ref[...]
        for j, blk in enumerate(blks):
            q = retp[r, blk]
            k = retp[r, BRANCH_WIDTH + j * LANES:BRANCH_WIDTH + (j + 1) * LANES]
            vj = retp[r, 2 * BRANCH_WIDTH + j * LANES:2 * BRANCH_WIDTH + (j + 1) * LANES]
            ret_qk.append(_dot_nt(_mx(_pair_split_rows(q, lo_mask)), _mx(k)))
            ret_kv.append(jnp.where(pair_diag, _dot(_mx(ks[:, blk].T), _mx(vj)), 0.0))
    gla_sc, gla_kv = [], []
    for c, r in enumerate(grw):
        for j in range(GLA_HEADS // 2):
            blk = blks[j]
            sc = _dot_nt(_mx(_pair_split_rows(gla_q[c][:, blk], lo_mask)), gla_kinv[c][:, blk])
            gla_sc.append(_mx(jnp.where(gla_causal, sc, 0.0)))
            for hh in range(2):
                hd = 2 * j + hh
                vh = glap[r, 2 * GLA_KEY + hd * LANES:2 * GLA_KEY + (hd + 1) * LANES]
                kvh = _dot(_mx(vh.T), gla_kst[c][:, blk])
                gla_kv.append(jnp.where(lo_mask if hh == 0 else jnp.logical_not(lo_mask), kvh, 0.0))

    ssd_intra = []
    for c, r in enumerate(rws):
        cum_t = cum[c].T
        ys = []
        for j in range(SSM_HEADS // 2):
            g = j // 2
            qkg = ssd_qk[c][g * CHUNK:(g + 1) * CHUNK, :]
            sc = []
            for hh in (2 * j, 2 * j + 1):
                seg = cum[c][:, hh:hh + 1] - cum_t[hh:hh + 1, :]
                sc.append(_mx(qkg * jnp.exp(jnp.where(causal, seg, NEG_BIG))))
            vj = ssd_v[c][:, blks[j]]
            ys.append(_dot(jnp.concatenate(sc, axis=1), _mx(_pair_split_rows(vj, lo_mask))))
        ssd_intra.append(jnp.concatenate(ys, axis=1))

    s_cur = ssd_state[...]
    for c, r in enumerate(rws):
        ecum_full = ssd_full[c][CHUNK:2 * CHUNK]
        yssm[r, :] = ssd_intra[c] + _dot(_mx(xact[r, col_c]), _mx(s_cur)) * ecum_full
        s_cur = s_cur * ecum_full[CHUNK - 1:CHUNK, :] + ssd_kv[c]
    ssd_state[...] = s_cur

    for j, blk in enumerate(blks):
        sj = ret_state[j]
        for c, r in enumerate(rws):
            sc = _mx(ret_qk[c * n_pairs + j] * ret_d_ref[j])
            qd = _mx(retp[r, blk] * ret_qdec_ref[:, blk])
            vj = retp[r, 2 * BRANCH_WIDTH + j * LANES:2 * BRANCH_WIDTH + (j + 1) * LANES]
            lhs = jnp.concatenate([sc[0:CHUNK], sc[CHUNK:2 * CHUNK], qd], axis=1)
            rhs = jnp.concatenate([_mx(_pair_split_rows(vj, lo_mask)), _mx(sj)], axis=0)
            yret[r, blk] = _dot(lhs, rhs)
            sj = sj * ret_cdec_ref[:, blk] + ret_kv[c * n_pairs + j]
        ret_state[j] = sj

    for hd in range(GLA_HEADS):
        j, hh = hd // 2, hd % 2
        blk = blks[j]
        st = gla_state[hd]
        for c, r in enumerate(grw):
            vh = glap[r, 2 * GLA_KEY + hd * LANES:2 * GLA_KEY + (hd + 1) * LANES]
            sc = gla_sc[c * (GLA_HEADS // 2) + j][hh * GLA_CHUNK:(hh + 1) * GLA_CHUNK]
            ygla[r, hd * LANES:(hd + 1) * LANES] = _dot(sc, _mx(vh)) + _dot_nt(_mx(gla_q[c][:, blk]), _mx(st))
            st = st * gla_elast[c][:, blk] + gla_kv[c * GLA_HEADS + hd]
        gla_state[hd] = st

    y = (yssm[...] + dfull_ref[...] * xact[:, 0:BRANCH_WIDTH]) * zs[...]
    parts = []
    gw = BRANCH_WIDTH // 2
    for g in range(2):
        yg = y[:, g * gw:(g + 1) * gw]
        parts.append(yg * lax.rsqrt(jnp.mean(yg * yg, axis=-1, keepdims=True) + GN_EPS))
    y_ssm = jnp.concatenate(parts, axis=1) * ssm_ng_ref[...]

    parts = []
    for j in range(RET_HEADS // 2):
        yb = yret[:, j * LANES:(j + 1) * LANES]
        s_all = jnp.sum(yb, axis=-1, keepdims=True)
        s_lo = jnp.sum(jnp.where(lo_mask, yb, 0.0), axis=-1, keepdims=True)
        yc = yb - jnp.where(lo_mask, s_lo, s_all - s_lo) * (1.0 / HALF)
        sq = yc * yc
        v_all = jnp.sum(sq, axis=-1, keepdims=True)
        v_lo = jnp.sum(jnp.where(lo_mask, sq, 0.0), axis=-1, keepdims=True)
        var = jnp.where(lo_mask, v_lo, v_all - v_lo) * (1.0 / HALF)
        parts.append(yc * lax.rsqrt(var + GN_EPS))
    y_ret = jnp.concatenate(parts, axis=1) * ret_ng_ref[...] * retp[:, 3 * BRANCH_WIDTH:]

    parts = []
    for hd in range(GLA_HEADS):
        yb = ygla[:, hd * LANES:(hd + 1) * LANES]
        parts.append(yb * lax.rsqrt(jnp.mean(yb * yb, axis=-1, keepdims=True) + GN_EPS))
    y_gla = jnp.concatenate(parts, axis=1) * gla_ng_ref[...] * glap[:, 2 * GLA_KEY + BRANCH_WIDTH:]

    h = h_s[...]
    merged = None
    for i, yb in enumerate((y_ssm, y_ret, y_gla)):
        cols = slice(i * D_MODEL, (i + 1) * D_MODEL)
        gate = _sigmoid(_dot(h, w_gate_ref[:, cols]) + bg_ref[:, cols])
        term = gate * _dot(_mx(yb), wb_ref[i])
        merged = term if merged is None else merged + term
    o_ref[...] = x_ref[...] + _dot(_mx(merged), wout_ref[...])


def _ffn_kernel(x_ref, ng_ref, wup_ref, cw_ref, cb_ref, wdown_ref, fg_ref, o_ref, u_buf, a_s, *, tm, final_norm):
    t = pl.program_id(1)

    @pl.when(t == 0)
    def _():
        u_buf[0:SUBLANES, :] = jnp.zeros((SUBLANES, 2 * D_FF), F32)

    x = x_ref[...]
    h = _mx(_rms_norm(x, ng_ref[...]))
    for j in range(D_FF // FFN_BW):
        acts = []
        for part in range(2):
            cols = slice(part * D_FF + j * FFN_BW, part * D_FF + (j + 1) * FFN_BW)
            u_buf[SUBLANES:SUBLANES + tm, cols] = _dot(h, wup_ref[:, cols])
            conv = cb_ref[:, cols]
            for k in range(FFN_CONV):
                off = SUBLANES - (FFN_CONV - 1) + k
                conv = conv + cw_ref[k:k + 1, cols] * u_buf[off:off + tm, cols]
            acts.append(conv)
        a_s[:, j * FFN_BW:(j + 1) * FFN_BW] = _mx(_silu(acts[0]) * acts[1])
    u_buf[0:SUBLANES, :] = u_buf[tm:tm + SUBLANES, :]
    ksplit = D_FF - FFN_BW
    y = x + _dot(a_s[:, :ksplit], wdown_ref[:ksplit, :]) + _dot(a_s[:, ksplit:], wdown_ref[ksplit:, :])
    if final_norm:
        y = _rms_norm(y, fg_ref[...])
    o_ref[...] = y


def _const_spec(shape):
    nd = len(shape)
    return pl.BlockSpec(shape, lambda b, t, _nd=nd: (0,) * _nd, pipeline_mode=pl.Buffered(1))


def _mixer_constants(seq):
    half = RET_HEAD_DIM // 2
    inv_freq = ROPE_BASE ** (-np.arange(half, dtype=np.float64) / half)
    ang = np.arange(seq, dtype=np.float64)[:, None] * inv_freq[None, :]
    lane = np.arange(LANES)
    cos = np.cos(ang)[:, lane % half]
    sin = np.sin(ang)[:, lane % half] * np.where((lane % RET_HEAD_DIM) < half, -1.0, 1.0)[None, :]

    log_gamma = np.log(1.0 - np.exp2(-5.0 - np.arange(RET_HEADS, dtype=np.float64)))
    pos = np.arange(CHUNK, dtype=np.float64)
    diff = pos[:, None] - pos[None, :]
    dmat = np.where(diff >= 0, np.exp(np.maximum(diff, 0.0)[None] * log_gamma[:, None, None]), 0.0)
    ret_d = dmat.reshape(RET_HEADS // 2, 2 * CHUNK, CHUNK)
    lg_full = np.repeat(log_gamma, RET_HEAD_DIM)
    ret_qdec = np.exp((pos[:, None] + 1.0) * lg_full[None, :])
    ret_wst = np.exp((CHUNK - 1.0 - pos[:, None]) * lg_full[None, :])
    ret_cdec = np.exp(CHUNK * lg_full)[None, :]

    expand = (np.arange(LANES)[:, None] == (np.arange(BRANCH_WIDTH)[None, :] // HALF)).astype(np.float64)
    e2 = np.concatenate([expand, expand], axis=0)
    tri = (pos[:, None] >= pos[None, :]).astype(np.float64)
    t2 = np.concatenate([tri, tri], axis=1)
    tri64 = tri[:GLA_CHUNK, :GLA_CHUNK]
    t64 = np.concatenate([tri64, tri64], axis=1)
    f = lambda a: jnp.asarray(a, F32)
    m = lambda a: jnp.asarray(a, MXU_DTYPE)
    return dict(cos=f(cos), sin=f(sin), ret_d=f(ret_d), ret_qdec=f(ret_qdec), ret_wst=f(ret_wst),
                ret_cdec=f(ret_cdec), e2=m(e2), t2=m(t2), t64=m(t64))


def _layer_spec(arr, layer):
    nd = arr.ndim - 1
    return pl.BlockSpec((None,) + arr.shape[1:], lambda b, t, _l=layer, _nd=nd: (_l,) + (0,) * _nd,
                        pipeline_mode=pl.Buffered(1))


def _rows(v):
    return v.reshape(v.shape[0], 1, -1).astype(F32)


def _pad_last(v, width):
    return jnp.pad(v, [(0, 0)] * (v.ndim - 1) + [(0, width - v.shape[-1])])


def _mixer_params(norm_mix_g, w_in, ssm_conv_w, ssm_conv_b, ssm_dt_bias, ssm_a_log, ssm_d, ssm_norm_g,
                  ret_norm_g, gla_w_alpha2, gla_b_alpha, gla_norm_g, w_branch, b_gate, w_out):
    seg = lambda lo, hi: w_in[:, :, lo:hi].astype(MXU_DTYPE)
    w_small = jnp.concatenate([_pad_last(w_in[:, :, _OFF_DT:_OFF_RET], LANES),
                               _pad_last(w_in[:, :, _OFF_GLR:_OFF_GATE], LANES)], axis=-1).astype(MXU_DTYPE)
    wa2 = jnp.pad(gla_w_alpha2, ((0, 0), (0, LANES - GLA_RANK), (0, 0))).astype(MXU_DTYPE)
    depth = w_in.shape[0]
    return dict(
        ng=_rows(norm_mix_g), w_ssm=seg(_OFF_Z, _OFF_DT), w_small=w_small, w_ret=seg(_OFF_RET, _OFF_GLA),
        w_gla=seg(_OFF_GLA, _OFF_GLR), w_gate=seg(_OFF_GATE, N_IN),
        cw=ssm_conv_w.astype(F32), cb=_rows(ssm_conv_b), dtb=_pad_last(_rows(ssm_dt_bias), LANES),
        alog=_pad_last(_rows(ssm_a_log), LANES), dfull=_rows(jnp.repeat(ssm_d, HALF, axis=-1)),
        ssm_ng=_rows(ssm_norm_g), ret_ng=_rows(ret_norm_g), wa2=wa2, ba=_rows(gla_b_alpha),
        gla_ng=_rows(gla_norm_g), wb=w_branch.astype(MXU_DTYPE), bg=_rows(b_gate.reshape(depth, -1)),
        wout=w_out.astype(MXU_DTYPE))


def _mix_layer(x, consts, p, layer):
    bsz, seq, d = x.shape
    tm = MIX_TM
    assert seq % tm == 0 and tm % CHUNK == 0 and d == D_MODEL
    lay = lambda name: (p[name], _layer_spec(p[name], layer))
    cst = lambda name: (consts[name], _const_spec(consts[name].shape))
    pos = lambda name: (consts[name], pl.BlockSpec((tm, LANES), lambda b, t: (t, 0)))
    operands = [
        (x, pl.BlockSpec((None, tm, d), lambda b, t: (b, t, 0))),
        lay("ng"), lay("w_ssm"), lay("w_small"), lay("w_ret"), lay("w_gla"), lay("w_gate"),
        lay("cw"), lay("cb"), lay("dtb"), lay("alog"), lay("dfull"), lay("ssm_ng"),
        pos("cos"), pos("sin"), cst("ret_d"), cst("ret_qdec"), cst("ret_wst"), cst("ret_cdec"),
        lay("ret_ng"), lay("wa2"), lay("ba"), lay("gla_ng"), lay("wb"), lay("bg"), lay("wout"),
        cst("e2"), cst("t2"), cst("t64"),
    ]
    scratch = [
        pltpu.VMEM((tm, d), MXU_DTYPE),
        pltpu.VMEM((SUBLANES + tm, SSM_CONV_DIM), F32),
        pltpu.VMEM((tm, SSM_CONV_DIM), F32),
        pltpu.VMEM((tm, BRANCH_WIDTH), F32),
        pltpu.VMEM((tm, LANES), F32),
        pltpu.VMEM((tm, LANES), F32),
        pltpu.VMEM((tm, 4 * BRANCH_WIDTH), F32),
        pltpu.VMEM((tm, 2 * GLA_KEY + 2 * BRANCH_WIDTH), F32),
        pltpu.VMEM((tm, GLA_KEY), F32),
        pltpu.VMEM((tm, BRANCH_WIDTH), F32),
        pltpu.VMEM((tm, BRANCH_WIDTH), F32),
        pltpu.VMEM((tm, BRANCH_WIDTH), F32),
        pltpu.VMEM((CHUNK, BRANCH_WIDTH), F32),
        pltpu.VMEM((RET_HEADS // 2, LANES, LANES), F32),
        pltpu.VMEM((GLA_HEADS, LANES, LANES), F32),
    ]
    return pl.pallas_call(
        functools.partial(_mix_kernel, tm=tm),
        grid=(bsz, seq // tm),
        in_specs=[s for _, s in operands],
        out_specs=pl.BlockSpec((None, tm, d), lambda b, t: (b, t, 0)),
        out_shape=jax.ShapeDtypeStruct(x.shape, x.dtype),
        scratch_shapes=scratch,
        compiler_params=pltpu.CompilerParams(dimension_semantics=("arbitrary", "arbitrary"),
                                             vmem_limit_bytes=VMEM_LIMIT_BYTES),
        name="mixer_layer",
    )(*[a for a, _ in operands])


def _ffn_params(norm_ffn_g, w_up, ffn_conv_w, ffn_conv_b, w_down):
    return dict(ng=_rows(norm_ffn_g), wup=w_up.astype(MXU_DTYPE), cw=ffn_conv_w.astype(F32),
                cb=_rows(ffn_conv_b), wdown=w_down.astype(MXU_DTYPE))


def _ffn_layer(x, p, final_g, layer, final_norm):
    bsz, seq, d = x.shape
    tm = FFN_TM
    assert seq % tm == 0 and d == D_MODEL and D_FF % FFN_BW == 0
    final_g = final_g.reshape(1, -1).astype(F32)
    names = ("ng", "wup", "cw", "cb", "wdown")
    args = [x] + [p[n] for n in names] + [final_g]
    specs = ([pl.BlockSpec((None, tm, d), lambda b, t: (b, t, 0))] + [_layer_spec(p[n], layer) for n in names]
             + [_const_spec(final_g.shape)])
    return pl.pallas_call(
        functools.partial(_ffn_kernel, tm=tm, final_norm=final_norm),
        grid=(bsz, seq // tm),
        in_specs=specs,
        out_specs=pl.BlockSpec((None, tm, d), lambda b, t: (b, t, 0)),
        out_shape=jax.ShapeDtypeStruct(x.shape, x.dtype),
        scratch_shapes=[pltpu.VMEM((SUBLANES + tm, 2 * D_FF), F32), pltpu.VMEM((tm, D_FF), MXU_DTYPE)],
        compiler_params=pltpu.CompilerParams(dimension_semantics=("arbitrary", "arbitrary"),
                                             vmem_limit_bytes=VMEM_LIMIT_BYTES),
        name="ffn_layer",
    )(*args)


def kernel(x, norm_mix_g, w_in, ssm_conv_w, ssm_conv_b, ssm_dt_bias, ssm_a_log, ssm_d, ssm_norm_g, ret_norm_g, gla_w_alpha2, gla_b_alpha, gla_norm_g, w_branch, b_gate, w_out, norm_ffn_g, w_up, ffn_conv_w, ffn_conv_b, w_down, norm_f_g):
    depth = w_in.shape[0]
    consts = _mixer_constants(x.shape[1])
    mix_p = _mixer_params(norm_mix_g, w_in, ssm_conv_w, ssm_conv_b, ssm_dt_bias, ssm_a_log, ssm_d, ssm_norm_g,
                          ret_norm_g, gla_w_alpha2, gla_b_alpha, gla_norm_g, w_branch, b_gate, w_out)
    ffn_p = _ffn_params(norm_ffn_g, w_up, ffn_conv_w, ffn_conv_b, w_down)
    for i in range(depth):
        x = _mix_layer(x, consts, mix_p, i)
        x = _ffn_layer(x, ffn_p, norm_f_g, i, final_norm=(i == depth - 1))
    return x
```

```python
import functools
import math

import numpy as np
import jax
import jax.numpy as jnp
from jax import lax
from jax.experimental import pallas as pl
from jax.experimental.pallas import tpu as pltpu

F32 = jnp.float32
MXU_DTYPE = jnp.bfloat16

D_MODEL = 1024
BRANCH_WIDTH = 512
N_BRANCH = 3
CHUNK = 128
GLA_CHUNK = 64
SSM_HEADS = 8
SSM_CONV = 4
SSM_CONV_DIM = 768
RET_HEADS = 8
RET_HEAD_DIM = 64
ROPE_BASE = 10000.0
GLA_HEADS = 4
GLA_KEY = 256
GLA_RANK = 16
GLA_GATE_NORMALIZER = 16.0
D_FF = 2816
FFN_CONV = 3
RMS_EPS = 1e-6
GN_EPS = 1e-5

_OFF_Z = 0
_OFF_XBC = 512
_OFF_DT = 1280
_OFF_RET = 1288
_OFF_GLA = 3336
_OFF_GLR = 4872
_OFF_GATE = 4888
N_IN = 7960

LANES = 128
SUBLANES = 8
HALF = 64
MIX_TM = 512
FFN_TM = 512
FFN_PHASES = 4
FFN_BW = 256
VMEM_LIMIT_BYTES = 60000 * 1024
NEG_BIG = -1e30


def _sigmoid(x):
    return 0.5 * jnp.tanh(0.5 * x) + 0.5


def _silu(x):
    hx = 0.5 * x
    return hx * (jnp.tanh(hx) + 1.0)


def _softplus(x):
    return jnp.maximum(x, 0.0) + jnp.log(1.0 + jnp.exp(-jnp.abs(x)))


def _split2(x):
    hi = x.astype(MXU_DTYPE)
    lo = (x - hi.astype(F32)).astype(MXU_DTYPE)
    return hi, lo


def _const_dot(c2, x):
    hi, lo = _split2(x)
    return jnp.dot(c2, jnp.concatenate([hi, lo], axis=0), preferred_element_type=F32)


def _dot_const(x, c2):
    hi, lo = _split2(x)
    return jnp.dot(jnp.concatenate([hi, lo], axis=1), c2, preferred_element_type=F32)


def _dot(a, b):
    return jnp.dot(a, b, preferred_element_type=F32)


def _dot_nt(a, b):
    return lax.dot_general(a, b, (((1,), (1,)), ((), ())), preferred_element_type=F32)


def _mx(x):
    return x.astype(MXU_DTYPE)


def _rms_norm(x, g):
    ms = jnp.mean(x * x, axis=-1, keepdims=True)
    return x * lax.rsqrt(ms + RMS_EPS) * g


def _pair_split_rows(x, lo_mask):
    zero = jnp.zeros_like(x)
    return jnp.concatenate([jnp.where(lo_mask, x, zero), jnp.where(lo_mask, zero, x)], axis=0)


def _mix_kernel(x_ref, ng_ref, w_ssm_ref, w_small_ref, w_ret_ref, w_gla_ref, w_gate_ref,
                cw_ref, cb_ref, dtb_ref, alog_ref, dfull_ref, ssm_ng_ref,
                cos_ref, sin_ref, ret_d_ref, ret_qdec_ref, ret_wst_ref, ret_cdec_ref, ret_ng_ref,
                wa2_ref, ba_ref, gla_ng_ref, wb_ref, bg_ref, wout_ref,
                e2_ref, t2_ref, t64_ref,
                o_ref,
                h_s, xbc_buf, xact, zs, dt_s, la_s, retp, glap, lag,
                yssm, yret, ygla, ssd_state, ret_state, gla_state, *, tm):
    t = pl.program_id(1)

    @pl.when(t == 0)
    def _():
        xbc_buf[0:SUBLANES, :] = jnp.zeros((SUBLANES, SSM_CONV_DIM), F32)
        ssd_state[...] = jnp.zeros_like(ssd_state)
        ret_state[...] = jnp.zeros_like(ret_state)
        gla_state[...] = jnp.zeros_like(gla_state)

    lane = lax.broadcasted_iota(jnp.int32, (1, LANES), 1)
    lo_mask = lane < HALF
    rot_first = (lane % HALF) < (HALF // 2)

    h = _mx(_rms_norm(x_ref[...], ng_ref[...]))
    h_s[...] = h

    zx = _dot(h, w_ssm_ref[...])
    zs[...] = _silu(zx[:, :BRANCH_WIDTH])
    xbc_buf[SUBLANES:SUBLANES + tm, :] = zx[:, BRANCH_WIDTH:]
    conv = cb_ref[...]
    for k in range(SSM_CONV):
        off = SUBLANES - (SSM_CONV - 1) + k
        conv = conv + cw_ref[k:k + 1, :] * xbc_buf[off:off + tm, :]
    xbc_buf[0:SUBLANES, :] = xbc_buf[tm:tm + SUBLANES, :]
    xact[...] = _silu(conv)

    small = _dot(h, w_small_ref[...])
    dt = _softplus(small[:, :LANES] + dtb_ref[...])
    dt_s[...] = dt
    la_s[...] = dt * (-jnp.exp(alog_ref[...]))

    rp = _dot(h, w_ret_ref[...])
    cos = cos_ref[...]
    sin = sin_ref[...]
    for j in range(2 * BRANCH_WIDTH // LANES):
        xb = rp[:, j * LANES:(j + 1) * LANES]
        partner = jnp.where(rot_first, pltpu.roll(xb, LANES - HALF // 2, 1), pltpu.roll(xb, HALF // 2, 1))
        rot = xb * cos + partner * sin
        if j >= BRANCH_WIDTH // LANES:
            rot = rot * (RET_HEAD_DIM ** -0.5)
        retp[:, j * LANES:(j + 1) * LANES] = rot
    retp[:, 2 * BRANCH_WIDTH:3 * BRANCH_WIDTH] = rp[:, 2 * BRANCH_WIDTH:3 * BRANCH_WIDTH]
    retp[:, 3 * BRANCH_WIDTH:] = _silu(rp[:, 3 * BRANCH_WIDTH:])

    gp = _dot(h, w_gla_ref[...])
    glap[:, 0:2 * GLA_KEY + BRANCH_WIDTH] = gp[:, 0:2 * GLA_KEY + BRANCH_WIDTH]
    glap[:, 2 * GLA_KEY + BRANCH_WIDTH:] = _silu(gp[:, 2 * GLA_KEY + BRANCH_WIDTH:])
    al = _dot(_mx(small[:, LANES:]), wa2_ref[...]) + ba_ref[...]
    lag[...] = (jnp.minimum(al, 0.0) - jnp.log(1.0 + jnp.exp(-jnp.abs(al)))) * (1.0 / GLA_GATE_NORMALIZER)

    row = lax.broadcasted_iota(jnp.int32, (CHUNK, CHUNK), 0)
    col = lax.broadcasted_iota(jnp.int32, (CHUNK, CHUNK), 1)
    causal = row >= col
    pair_diag = (row // HALF) == (col // HALF)
    srow = lax.broadcasted_iota(jnp.int32, (CHUNK, BRANCH_WIDTH), 0)
    scol = lax.broadcasted_iota(jnp.int32, (CHUNK, BRANCH_WIDTH), 1)
    ssd_diag = (srow // HALF) == (scol // (BRANCH_WIDTH // 2))
    grow = lax.broadcasted_iota(jnp.int32, (2 * GLA_CHUNK, GLA_CHUNK), 0)
    gcol = lax.broadcasted_iota(jnp.int32, (2 * GLA_CHUNK, GLA_CHUNK), 1)
    gla_causal = (grow % GLA_CHUNK) >= gcol

    nch = tm // CHUNK
    ngc = tm // GLA_CHUNK
    n_pairs = BRANCH_WIDTH // LANES
    rws = [slice(c * CHUNK, (c + 1) * CHUNK) for c in range(nch)]
    grw = [slice(c * GLA_CHUNK, (c + 1) * GLA_CHUNK) for c in range(ngc)]
    blks = [slice(j * LANES, (j + 1) * LANES) for j in range(n_pairs)]
    col_b = slice(BRANCH_WIDTH, BRANCH_WIDTH + LANES)
    col_c = slice(BRANCH_WIDTH + LANES, BRANCH_WIDTH + 2 * LANES)


    cum = [_const_dot(t2_ref[...], la_s[r, :]) for r in rws]
    cumg = [_const_dot(t64_ref[...], lag[r, :]) for r in grw]

    ssd_full = []
    for c, r in enumerate(rws):
        ecum = jnp.exp(cum[c])
        wst = jnp.exp(cum[c][CHUNK - 1:CHUNK, :] - cum[c])
        ssd_full.append(_dot_const(jnp.concatenate([dt_s[r, :], ecum, wst], axis=0), e2_ref[...]))
    gla_q, gla_kinv, gla_kst, gla_elast = [], [], [], []
    for c, r in enumerate(grw):
        last = cumg[c][GLA_CHUNK - 1:GLA_CHUNK, :]
        gk = glap[r, GLA_KEY:2 * GLA_KEY]
        gla_q.append(glap[r, 0:GLA_KEY] * (HALF ** -0.5) * jnp.exp(cumg[c]))
        gla_kinv.append(_mx(gk * jnp.exp(-cumg[c])))
        gla_kst.append(gk * jnp.exp(last - cumg[c]))
        gla_elast.append(jnp.broadcast_to(jnp.exp(last), (SUBLANES, GLA_KEY)))

    ssd_v, ssd_qk, ssd_kv = [], [], []
    for c, r in enumerate(rws):
        v = xact[r, 0:BRANCH_WIDTH] * ssd_full[c][0:CHUNK]
        ssd_v.append(v)
        bm = xact[r, col_b]
        ssd_qk.append(_dot_nt(_pair_split_rows(_mx(xact[r, col_c]), lo_mask), _mx(bm)))
        kv = _dot(_mx(bm.T), _mx(v * ssd_full[c][2 * CHUNK:3 * CHUNK]))
        ssd_kv.append(jnp.where(ssd_diag, kv, 0.0))
    ret_qk, ret_kv = [], []
    for c, r in enumerate(rws):
        ks = retp[r, BRANCH_WIDTH:2 * BRANCH_WIDTH] * ret_wst_ref[...]
        for j, blk in enumerate(blks):
            q = retp[r, blk]
            k = retp[r, BRANCH_WIDTH + j * LANES:BRANCH_WIDTH + (j + 1) * LANES]
            vj = retp[r, 2 * BRANCH_WIDTH + j * LANES:2 * BRANCH_WIDTH + (j + 1) * LANES]
            ret_qk.append(_dot_nt(_pair_split_rows(_mx(q), lo_mask), _mx(k)))
            ret_kv.append(jnp.where(pair_diag, _dot(_mx(ks[:, blk].T), _mx(vj)), 0.0))
    gla_qs, gla_sc, gla_kv, gla_dec = [], [], [], []
    for c, r in enumerate(grw):
        for j in range(GLA_HEADS // 2):
            blk = blks[j]
            qs = _pair_split_rows(_mx(gla_q[c][:, blk]), lo_mask)
            gla_qs.append(qs)
            sc = _dot_nt(qs, gla_kinv[c][:, blk])
            gla_sc.append(_mx(jnp.where(gla_causal, sc, 0.0)))
            kst_t = _mx(gla_kst[c][:, blk].T)
            kvs = []
            for hh in range(2):
                hd = 2 * j + hh
                vh = _mx(glap[r, 2 * GLA_KEY + hd * LANES:2 * GLA_KEY + (hd + 1) * LANES])
                kvs.append(_dot(kst_t[hh * HALF:(hh + 1) * HALF], vh))
            gla_kv.append(jnp.concatenate(kvs, axis=0))
            gla_dec.append(gla_elast[c][:, blk].T[:, 0:1])

    ssd_intra = []
    for c, r in enumerate(rws):
        cum_t = cum[c].T
        ys = []
        for j in range(SSM_HEADS // 2):
            g = j // 2
            qkg = ssd_qk[c][g * CHUNK:(g + 1) * CHUNK, :]
            sc = []
            for hh in (2 * j, 2 * j + 1):
                seg = cum[c][:, hh:hh + 1] - cum_t[hh:hh + 1, :]
                sc.append(_mx(qkg * jnp.exp(jnp.where(causal, seg, NEG_BIG))))
            vj = ssd_v[c][:, blks[j]]
            ys.append(_dot(jnp.concatenate(sc, axis=1), _pair_split_rows(_mx(vj), lo_mask)))
        ssd_intra.append(jnp.concatenate(ys, axis=1))

    s_cur = ssd_state[...]
    for c, r in enumerate(rws):
        ecum_full = ssd_full[c][CHUNK:2 * CHUNK]
        yssm[r, :] = ssd_intra[c] + _dot(_mx(xact[r, col_c]), _mx(s_cur)) * ecum_full
        s_cur = s_cur * ecum_full[CHUNK - 1:CHUNK, :] + ssd_kv[c]
    ssd_state[...] = s_cur

    for j, blk in enumerate(blks):
        sj = ret_state[j]
        for c, r in enumerate(rws):
            sc = _mx(ret_qk[c * n_pairs + j] * ret_d_ref[j])
            qd = _mx(retp[r, blk] * ret_qdec_ref[:, blk])
            vj = retp[r, 2 * BRANCH_WIDTH + j * LANES:2 * BRANCH_WIDTH + (j + 1) * LANES]
            lhs = jnp.concatenate([sc[0:CHUNK], sc[CHUNK:2 * CHUNK], qd], axis=1)
            rhs = jnp.concatenate([_pair_split_rows(_mx(vj), lo_mask), _mx(sj)], axis=0)
            yret[r, blk] = _dot(lhs, rhs)
            sj = sj * ret_cdec_ref[:, blk] + ret_kv[c * n_pairs + j]
        ret_state[j] = sj

    for j in range(GLA_HEADS // 2):
        sp = gla_state[j]
        for c, r in enumerate(grw):
            i = c * (GLA_HEADS // 2) + j
            rhs_state = _mx(sp)
            for hh in range(2):
                hd = 2 * j + hh
                hr = slice(hh * GLA_CHUNK, (hh + 1) * GLA_CHUNK)
                vh = _mx(glap[r, 2 * GLA_KEY + hd * LANES:2 * GLA_KEY + (hd + 1) * LANES])
                lhs = jnp.concatenate([gla_qs[i][hr], gla_sc[i][hr]], axis=1)
                ygla[r, hd * LANES:(hd + 1) * LANES] = _dot(lhs, jnp.concatenate([rhs_state, vh], axis=0))
            sp = sp * gla_dec[i] + gla_kv[i]
        gla_state[j] = sp

    y = (yssm[...] + dfull_ref[...] * xact[:, 0:BRANCH_WIDTH]) * zs[...]
    parts = []
    gw = BRANCH_WIDTH // 2
    for g in range(2):
        yg = y[:, g * gw:(g + 1) * gw]
        parts.append(yg * lax.rsqrt(jnp.mean(yg * yg, axis=-1, keepdims=True) + GN_EPS))
    y_ssm = jnp.concatenate(parts, axis=1) * ssm_ng_ref[...]

    parts = []
    for j in range(RET_HEADS // 2):
        yb = yret[:, j * LANES:(j + 1) * LANES]
        s_all = jnp.sum(yb, axis=-1, keepdims=True)
        s_lo = jnp.sum(jnp.where(lo_mask, yb, 0.0), axis=-1, keepdims=True)
        yc = yb - jnp.where(lo_mask, s_lo, s_all - s_lo) * (1.0 / HALF)
        sq = yc * yc
        v_all = jnp.sum(sq, axis=-1, keepdims=True)
        v_lo = jnp.sum(jnp.where(lo_mask, sq, 0.0), axis=-1, keepdims=True)
        var = jnp.where(lo_mask, v_lo, v_all - v_lo) * (1.0 / HALF)
        parts.append(yc * lax.rsqrt(var + GN_EPS))
    y_ret = jnp.concatenate(parts, axis=1) * ret_ng_ref[...] * retp[:, 3 * BRANCH_WIDTH:]

    parts = []
    for hd in range(GLA_HEADS):
        yb = ygla[:, hd * LANES:(hd + 1) * LANES]
        parts.append(yb * lax.rsqrt(jnp.mean(yb * yb, axis=-1, keepdims=True) + GN_EPS))
    y_gla = jnp.concatenate(parts, axis=1) * gla_ng_ref[...] * glap[:, 2 * GLA_KEY + BRANCH_WIDTH:]

    h = h_s[...]
    merged = None
    for i, yb in enumerate((y_ssm, y_ret, y_gla)):
        cols = slice(i * D_MODEL, (i + 1) * D_MODEL)
        gate = _sigmoid(_dot(h, w_gate_ref[:, cols]) + bg_ref[:, cols])
        term = gate * _dot(_mx(yb), wb_ref[i])
        merged = term if merged is None else merged + term
    o_ref[...] = x_ref[...] + _dot(_mx(merged), wout_ref[...])


def _ffn_kernel(x_ref, ng_ref, wup_ref, cw_ref, cb_ref, wdown_ref, fg_ref, o_ref, hist_buf, a_s, row_s, *, tm, final_norm):
    t = pl.program_id(1)
    gr = tm // FFN_PHASES
    reg = SUBLANES + gr
    win = [slice(i * reg + SUBLANES - 1, i * reg + SUBLANES - 1 + gr) for i in range(2)]
    body = [slice(i * reg + SUBLANES, (i + 1) * reg) for i in range(2)]
    head = [slice(i * reg, i * reg + SUBLANES) for i in range(2)]
    tail = [slice((i + 1) * reg - SUBLANES, (i + 1) * reg) for i in range(2)]

    @pl.when(t == 0)
    def _():
        for i in range(2):
            hist_buf[head[i], :] = jnp.zeros((SUBLANES, 2 * D_FF), F32)

    nlb = D_MODEL // LANES
    for c in range(nlb):
        row_s[c] = x_ref[:, c * LANES:(c + 1) * LANES]
    x = jnp.concatenate(
        [jnp.concatenate([row_s[c, pl.ds(p, gr, stride=FFN_PHASES), :] for c in range(nlb)], axis=1)
         for p in range(FFN_PHASES)], axis=0)
    h = _mx(_rms_norm(x, ng_ref[...]))
    for j in range(D_FF // FFN_BW):
        acts = []
        for part in range(2):
            cols = slice(part * D_FF + j * FFN_BW, part * D_FF + (j + 1) * FFN_BW)
            u = _dot(h, wup_ref[:, cols])
            cur = [u[p * gr:(p + 1) * gr] for p in range(FFN_PHASES)]
            for i in range(2):
                hist_buf[body[i], cols] = cur[FFN_PHASES - 2 + i]
            w2 = hist_buf[win[0], cols]
            w3 = hist_buf[win[1], cols]
            prev1 = [w3, cur[0], cur[1], cur[2]]
            prev2 = [w2, w3, cur[0], cur[1]]
            out = []
            for p in range(FFN_PHASES):
                out.append(cb_ref[:, cols] + cw_ref[2:3, cols] * cur[p] + cw_ref[1:2, cols] * prev1[p]
                           + cw_ref[0:1, cols] * prev2[p])
            acts.append(jnp.concatenate(out, axis=0))
        a_s[:, j * FFN_BW:(j + 1) * FFN_BW] = _mx(_silu(acts[0]) * acts[1])
    for i in range(2):
        hist_buf[head[i], :] = hist_buf[tail[i], :]
    ksplit = D_FF - 2 * FFN_BW
    y = x + _dot(a_s[:, :ksplit], wdown_ref[:ksplit, :]) + _dot(a_s[:, ksplit:], wdown_ref[ksplit:, :])
    if final_norm:
        y = _rms_norm(y, fg_ref[...])
    for c in range(nlb):
        for p in range(FFN_PHASES):
            row_s[c, pl.ds(p, gr, stride=FFN_PHASES), :] = y[p * gr:(p + 1) * gr, c * LANES:(c + 1) * LANES]
    for c in range(nlb):
        o_ref[:, c * LANES:(c + 1) * LANES] = row_s[c]


def _const_spec(shape):
    nd = len(shape)
    return pl.BlockSpec(shape, lambda b, t, _nd=nd: (0,) * _nd, pipeline_mode=pl.Buffered(1))


def _mixer_constants(seq):
    half = RET_HEAD_DIM // 2
    inv_freq = ROPE_BASE ** (-np.arange(half, dtype=np.float64) / half)
    ang = np.arange(seq, dtype=np.float64)[:, None] * inv_freq[None, :]
    lane = np.arange(LANES)
    cos = np.cos(ang)[:, lane % half]
    sin = np.sin(ang)[:, lane % half] * np.where((lane % RET_HEAD_DIM) < half, -1.0, 1.0)[None, :]

    log_gamma = np.log(1.0 - np.exp2(-5.0 - np.arange(RET_HEADS, dtype=np.float64)))
    pos = np.arange(CHUNK, dtype=np.float64)
    diff = pos[:, None] - pos[None, :]
    dmat = np.where(diff >= 0, np.exp(np.maximum(diff, 0.0)[None] * log_gamma[:, None, None]), 0.0)
    ret_d = dmat.reshape(RET_HEADS // 2, 2 * CHUNK, CHUNK)
    lg_full = np.repeat(log_gamma, RET_HEAD_DIM)
    ret_qdec = np.exp((pos[:, None] + 1.0) * lg_full[None, :])
    ret_wst = np.exp((CHUNK - 1.0 - pos[:, None]) * lg_full[None, :])
    ret_cdec = np.exp(CHUNK * lg_full)[None, :]

    expand = (np.arange(LANES)[:, None] == (np.arange(BRANCH_WIDTH)[None, :] // HALF)).astype(np.float64)
    e2 = np.concatenate([expand, expand], axis=0)
    tri = (pos[:, None] >= pos[None, :]).astype(np.float64)
    t2 = np.concatenate([tri, tri], axis=1)
    tri64 = tri[:GLA_CHUNK, :GLA_CHUNK]
    t64 = np.concatenate([tri64, tri64], axis=1)
    f = lambda a: jnp.asarray(a, F32)
    m = lambda a: jnp.asarray(a, MXU_DTYPE)
    return dict(cos=f(cos), sin=f(sin), ret_d=f(ret_d), ret_qdec=f(ret_qdec), ret_wst=f(ret_wst),
                ret_cdec=f(ret_cdec), e2=m(e2), t2=m(t2), t64=m(t64))


def _layer_spec(arr, layer):
    nd = arr.ndim - 1
    return pl.BlockSpec((None,) + arr.shape[1:], lambda b, t, _l=layer, _nd=nd: (_l,) + (0,) * _nd,
                        pipeline_mode=pl.Buffered(1))


def _rows(v):
    return v.reshape(v.shape[0], 1, -1).astype(F32)


def _pad_last(v, width):
    return jnp.pad(v, [(0, 0)] * (v.ndim - 1) + [(0, width - v.shape[-1])])


def _mixer_params(norm_mix_g, w_in, ssm_conv_w, ssm_conv_b, ssm_dt_bias, ssm_a_log, ssm_d, ssm_norm_g,
                  ret_norm_g, gla_w_alpha2, gla_b_alpha, gla_norm_g, w_branch, b_gate, w_out):
    seg = lambda lo, hi: w_in[:, :, lo:hi].astype(MXU_DTYPE)
    w_small = jnp.concatenate([_pad_last(w_in[:, :, _OFF_DT:_OFF_RET], LANES),
                               _pad_last(w_in[:, :, _OFF_GLR:_OFF_GATE], LANES)], axis=-1).astype(MXU_DTYPE)
    wa2 = jnp.pad(gla_w_alpha2, ((0, 0), (0, LANES - GLA_RANK), (0, 0))).astype(MXU_DTYPE)
    depth = w_in.shape[0]
    return dict(
        ng=_rows(norm_mix_g), w_ssm=seg(_OFF_Z, _OFF_DT), w_small=w_small, w_ret=seg(_OFF_RET, _OFF_GLA),
        w_gla=seg(_OFF_GLA, _OFF_GLR), w_gate=seg(_OFF_GATE, N_IN),
        cw=ssm_conv_w.astype(F32), cb=_rows(ssm_conv_b), dtb=_pad_last(_rows(ssm_dt_bias), LANES),
        alog=_pad_last(_rows(ssm_a_log), LANES), dfull=_rows(jnp.repeat(ssm_d, HALF, axis=-1)),
        ssm_ng=_rows(ssm_norm_g), ret_ng=_rows(ret_norm_g), wa2=wa2, ba=_rows(gla_b_alpha),
        gla_ng=_rows(gla_norm_g), wb=w_branch.astype(MXU_DTYPE), bg=_rows(b_gate.reshape(depth, -1)),
        wout=w_out.astype(MXU_DTYPE))


def _mix_layer(x, consts, p, layer):
    bsz, seq, d = x.shape
    tm = MIX_TM
    assert seq % tm == 0 and tm % CHUNK == 0 and d == D_MODEL
    lay = lambda name: (p[name], _layer_spec(p[name], layer))
    cst = lambda name: (consts[name], _const_spec(consts[name].shape))
    pos = lambda name: (consts[name], pl.BlockSpec((tm, LANES), lambda b, t: (t, 0)))
    operands = [
        (x, pl.BlockSpec((None, tm, d), lambda b, t: (b, t, 0))),
        lay("ng"), lay("w_ssm"), lay("w_small"), lay("w_ret"), lay("w_gla"), lay("w_gate"),
        lay("cw"), lay("cb"), lay("dtb"), lay("alog"), lay("dfull"), lay("ssm_ng"),
        pos("cos"), pos("sin"), cst("ret_d"), cst("ret_qdec"), cst("ret_wst"), cst("ret_cdec"),
        lay("ret_ng"), lay("wa2"), lay("ba"), lay("gla_ng"), lay("wb"), lay("bg"), lay("wout"),
        cst("e2"), cst("t2"), cst("t64"),
    ]
    scratch = [
        pltpu.VMEM((tm, d), MXU_DTYPE),
        pltpu.VMEM((SUBLANES + tm, SSM_CONV_DIM), F32),
        pltpu.VMEM((tm, SSM_CONV_DIM), F32),
        pltpu.VMEM((tm, BRANCH_WIDTH), F32),
        pltpu.VMEM((tm, LANES), F32),
        pltpu.VMEM((tm, LANES), F32),
        pltpu.VMEM((tm, 4 * BRANCH_WIDTH), F32),
        pltpu.VMEM((tm, 2 * GLA_KEY + 2 * BRANCH_WIDTH), F32),
        pltpu.VMEM((tm, GLA_KEY), F32),
        pltpu.VMEM((tm, BRANCH_WIDTH), F32),
        pltpu.VMEM((tm, BRANCH_WIDTH), F32),
        pltpu.VMEM((tm, BRANCH_WIDTH), F32),
        pltpu.VMEM((CHUNK, BRANCH_WIDTH), F32),
        pltpu.VMEM((RET_HEADS // 2, LANES, LANES), F32),
        pltpu.VMEM((GLA_HEADS // 2, LANES, LANES), F32),
    ]
    return pl.pallas_call(
        functools.partial(_mix_kernel, tm=tm),
        grid=(bsz, seq // tm),
        in_specs=[s for _, s in operands],
        out_specs=pl.BlockSpec((None, tm, d), lambda b, t: (b, t, 0)),
        out_shape=jax.ShapeDtypeStruct(x.shape, x.dtype),
        scratch_shapes=scratch,
        compiler_params=pltpu.CompilerParams(dimension_semantics=("arbitrary", "arbitrary"),
                                             vmem_limit_bytes=VMEM_LIMIT_BYTES),
        name="mixer_layer",
    )(*[a for a, _ in operands])


def _ffn_params(norm_ffn_g, w_up, ffn_conv_w, ffn_conv_b, w_down):
    return dict(ng=_rows(norm_ffn_g), wup=w_up.astype(MXU_DTYPE), cw=ffn_conv_w.astype(F32),
                cb=_rows(ffn_conv_b), wdown=w_down.astype(MXU_DTYPE))


def _ffn_layer(x, p, final_g, layer, final_norm):
    bsz, seq, d = x.shape
    tm = FFN_TM
    assert seq % tm == 0 and d == D_MODEL and D_FF % FFN_BW == 0 and FFN_CONV == 3 and FFN_PHASES == 4
    final_g = final_g.reshape(1, -1).astype(F32)
    names = ("ng", "wup", "cw", "cb", "wdown")
    args = [x] + [p[n] for n in names] + [final_g]
    specs = ([pl.BlockSpec((None, tm, d), lambda b, t: (b, t, 0))] + [_layer_spec(p[n], layer) for n in names]
             + [_const_spec(final_g.shape)])
    return pl.pallas_call(
        functools.partial(_ffn_kernel, tm=tm, final_norm=final_norm),
        grid=(bsz, seq // tm),
        in_specs=specs,
        out_specs=pl.BlockSpec((None, tm, d), lambda b, t: (b, t, 0)),
        out_shape=jax.ShapeDtypeStruct(x.shape, x.dtype),
        scratch_shapes=[pltpu.VMEM((2 * (SUBLANES + tm // FFN_PHASES), 2 * D_FF), F32), pltpu.VMEM((tm, D_FF), MXU_DTYPE),
                        pltpu.VMEM((D_MODEL // LANES, tm, LANES), F32)],
        compiler_params=pltpu.CompilerParams(dimension_semantics=("arbitrary", "arbitrary"),
                                             vmem_limit_bytes=VMEM_LIMIT_BYTES),
        name="ffn_layer",
    )(*args)


def kernel(x, norm_mix_g, w_in, ssm_conv_w, ssm_conv_b, ssm_dt_bias, ssm_a_log, ssm_d, ssm_norm_g, ret_norm_g, gla_w_alpha2, gla_b_alpha, gla_norm_g, w_branch, b_gate, w_out, norm_ffn_g, w_up, ffn_conv_w, ffn_conv_b, w_down, norm_f_g):
    depth = w_in.shape[0]
    consts = _mixer_constants(x.shape[1])
    mix_p = _mixer_params(norm_mix_g, w_in, ssm_conv_w, ssm_conv_b, ssm_dt_bias, ssm_a_log, ssm_d, ssm_norm_g,
                          ret_norm_g, gla_w_alpha2, gla_b_alpha, gla_norm_g, w_branch, b_gate, w_out)
    ffn_p = _ffn_params(norm_ffn_g, w_up, ffn_conv_w, ffn_conv_b, w_down)
    for i in range(depth):
        x = _mix_layer(x, consts, mix_p, i)
        x = _ffn_layer(x, ffn_p, norm_f_g, i, final_norm=(i == depth - 1))
    return x
```

```python
import functools
import math

import numpy as np
import jax
import jax.numpy as jnp
from jax import lax
from jax.experimental import pallas as pl
from jax.experimental.pallas import tpu as pltpu

F32 = jnp.float32
MXU_DTYPE = jnp.bfloat16

D_MODEL = 1024
BRANCH_WIDTH = 512
N_BRANCH = 3
CHUNK = 128
GLA_CHUNK = 64
SSM_HEADS = 8
SSM_CONV = 4
SSM_CONV_DIM = 768
RET_HEADS = 8
RET_HEAD_DIM = 64
ROPE_BASE = 10000.0
GLA_HEADS = 4
GLA_KEY = 256
GLA_RANK = 16
GLA_GATE_NORMALIZER = 16.0
D_FF = 2816
FFN_CONV = 3
RMS_EPS = 1e-6
GN_EPS = 1e-5

_OFF_Z = 0
_OFF_XBC = 512
_OFF_DT = 1280
_OFF_RET = 1288
_OFF_GLA = 3336
_OFF_GLR = 4872
_OFF_GATE = 4888
N_IN = 7960
_W_RET = _OFF_GLA - _OFF_RET
_W_SMALL_OFF = _W_RET
_W_GATE = N_IN - _OFF_GATE
_W_GATE_OFF = _W_GATE
_W_GLA = _OFF_GLR - _OFF_GLA
_W_GLA_OFF = _W_GATE_OFF + _W_GATE
_W_SSM = _OFF_DT - _OFF_Z
_W_SSM_OFF = _W_GLA_OFF + _W_GLA

LANES = 128
SUBLANES = 8
HALF = 64
MIX_TM = 512
FFN_TM = 512
FFN_PHASES = 4
FFN_BW = 256
VMEM_LIMIT_BYTES = 60000 * 1024
NEG_BIG = -1e30


def _sigmoid(x):
    return 0.5 * jnp.tanh(0.5 * x) + 0.5


def _silu(x):
    hx = 0.5 * x
    return hx * (jnp.tanh(hx) + 1.0)


def _softplus(x):
    return jnp.maximum(x, 0.0) + jnp.log(1.0 + jnp.exp(-jnp.abs(x)))


def _split2(x):
    hi = x.astype(MXU_DTYPE)
    lo = (x - hi.astype(F32)).astype(MXU_DTYPE)
    return hi, lo


def _const_dot(c2, x):
    hi, lo = _split2(x)
    return jnp.dot(c2, jnp.concatenate([hi, lo], axis=0), preferred_element_type=F32)


def _dot_const(x, c2):
    hi, lo = _split2(x)
    return jnp.dot(jnp.concatenate([hi, lo], axis=1), c2, preferred_element_type=F32)


def _dot(a, b):
    return jnp.dot(a, b, preferred_element_type=F32)


def _dot_nt(a, b):
    return lax.dot_general(a, b, (((1,), (1,)), ((), ())), preferred_element_type=F32)


def _mx(x):
    return x.astype(MXU_DTYPE)


def _rms_norm(x, g):
    ms = jnp.mean(x * x, axis=-1, keepdims=True)
    return x * lax.rsqrt(ms + RMS_EPS) * g


def _pair_split_rows(x, lo_mask):
    zero = jnp.zeros_like(x)
    return jnp.concatenate([jnp.where(lo_mask, x, zero), jnp.where(lo_mask, zero, x)], axis=0)


def _mix_kernel(x_ref, ng_ref, w_ssm_ref, w_small_ref, w_ret_ref, w_gla_ref, w_gate_ref,
                cw_ref, cb_ref, dtb_ref, alog_ref, dfull_ref, ssm_ng_ref,
                cos_ref, sin_ref, ret_d_ref, ret_qdec_ref, ret_wst_ref, ret_cdec_ref, ret_ng_ref,
                wa2_ref, ba_ref, gla_ng_ref, wb_ref, bg_ref, wout_ref,
                e2_ref, t2_ref, t64_ref,
                o_ref,
                h_s, xbc_buf, xact, zs, dt_s, la_s, retp, glap, lag,
                yssm, yret, ygla, ssd_state, ret_state, gla_state, *, tm):
    t = pl.program_id(1)

    @pl.when(t == 0)
    def _():
        xbc_buf[0:SUBLANES, :] = jnp.zeros((SUBLANES, SSM_CONV_DIM), F32)
        ssd_state[...] = jnp.zeros_like(ssd_state)
        ret_state[...] = jnp.zeros_like(ret_state)
        gla_state[...] = jnp.zeros_like(gla_state)

    lane = lax.broadcasted_iota(jnp.int32, (1, LANES), 1)
    lo_mask = lane < HALF
    rot_first = (lane % HALF) < (HALF // 2)

    h = _mx(_rms_norm(x_ref[...], ng_ref[...]))
    h_s[...] = h

    zx = _dot(h, w_ssm_ref[...])
    zs[...] = _silu(zx[:, :BRANCH_WIDTH])
    xbc_buf[SUBLANES:SUBLANES + tm, :] = zx[:, BRANCH_WIDTH:]
    conv = cb_ref[...]
    for k in range(SSM_CONV):
        off = SUBLANES - (SSM_CONV - 1) + k
        conv = conv + cw_ref[k:k + 1, :] * xbc_buf[off:off + tm, :]
    xbc_buf[0:SUBLANES, :] = xbc_buf[tm:tm + SUBLANES, :]
    xact[...] = _silu(conv)

    small = _dot(h, w_small_ref[...])
    dt = _softplus(small[:, :LANES] + dtb_ref[...])
    dt_s[...] = dt
    la_s[...] = dt * (-jnp.exp(alog_ref[...]))

    rp = _dot(h, w_ret_ref[...])
    cos = cos_ref[...]
    sin = sin_ref[...]
    for j in range(2 * BRANCH_WIDTH // LANES):
        xb = rp[:, j * LANES:(j + 1) * LANES]
        partner = jnp.where(rot_first, pltpu.roll(xb, LANES - HALF // 2, 1), pltpu.roll(xb, HALF // 2, 1))
        rot = xb * cos + partner * sin
        if j >= BRANCH_WIDTH // LANES:
            rot = rot * (RET_HEAD_DIM ** -0.5)
        retp[:, j * LANES:(j + 1) * LANES] = rot
    retp[:, 2 * BRANCH_WIDTH:3 * BRANCH_WIDTH] = rp[:, 2 * BRANCH_WIDTH:3 * BRANCH_WIDTH]
    retp[:, 3 * BRANCH_WIDTH:] = _silu(rp[:, 3 * BRANCH_WIDTH:])

    gp = _dot(h, w_gla_ref[...])
    glap[:, 0:2 * GLA_KEY + BRANCH_WIDTH] = gp[:, 0:2 * GLA_KEY + BRANCH_WIDTH]
    glap[:, 2 * GLA_KEY + BRANCH_WIDTH:] = _silu(gp[:, 2 * GLA_KEY + BRANCH_WIDTH:])
    al = _dot(_mx(small[:, LANES:]), wa2_ref[...]) + ba_ref[...]
    lag[...] = (jnp.minimum(al, 0.0) - jnp.log(1.0 + jnp.exp(-jnp.abs(al)))) * (1.0 / GLA_GATE_NORMALIZER)

    row = lax.broadcasted_iota(jnp.int32, (CHUNK, CHUNK), 0)
    col = lax.broadcasted_iota(jnp.int32, (CHUNK, CHUNK), 1)
    causal = row >= col
    pair_diag = (row // HALF) == (col // HALF)
    srow = lax.broadcasted_iota(jnp.int32, (CHUNK, BRANCH_WIDTH), 0)
    scol = lax.broadcasted_iota(jnp.int32, (CHUNK, BRANCH_WIDTH), 1)
    ssd_diag = (srow // HALF) == (scol // (BRANCH_WIDTH // 2))
    grow = lax.broadcasted_iota(jnp.int32, (2 * GLA_CHUNK, GLA_CHUNK), 0)
    gcol = lax.broadcasted_iota(jnp.int32, (2 * GLA_CHUNK, GLA_CHUNK), 1)
    gla_causal = (grow % GLA_CHUNK) >= gcol

    nch = tm // CHUNK
    ngc = tm // GLA_CHUNK
    n_pairs = BRANCH_WIDTH // LANES
    rws = [slice(c * CHUNK, (c + 1) * CHUNK) for c in range(nch)]
    grw = [slice(c * GLA_CHUNK, (c + 1) * GLA_CHUNK) for c in range(ngc)]
    blks = [slice(j * LANES, (j + 1) * LANES) for j in range(n_pairs)]
    col_b = slice(BRANCH_WIDTH, BRANCH_WIDTH + LANES)
    col_c = slice(BRANCH_WIDTH + LANES, BRANCH_WIDTH + 2 * LANES)


    cum = [_const_dot(t2_ref[...], la_s[r, :]) for r in rws]
    cumg = [_const_dot(t64_ref[...], lag[r, :]) for r in grw]

    ssd_full = []
    for c, r in enumerate(rws):
        ecum = jnp.exp(cum[c])
        wst = jnp.exp(cum[c][CHUNK - 1:CHUNK, :] - cum[c])
        ssd_full.append(_dot_const(jnp.concatenate([dt_s[r, :], ecum, wst], axis=0), e2_ref[...]))
    gla_q, gla_kinv, gla_kst, gla_elast = [], [], [], []
    for c, r in enumerate(grw):
        last = cumg[c][GLA_CHUNK - 1:GLA_CHUNK, :]
        gk = glap[r, GLA_KEY:2 * GLA_KEY]
        gla_q.append(glap[r, 0:GLA_KEY] * (HALF ** -0.5) * jnp.exp(cumg[c]))
        gla_kinv.append(_mx(gk * jnp.exp(-cumg[c])))
        gla_kst.append(gk * jnp.exp(last - cumg[c]))
        gla_elast.append(jnp.broadcast_to(jnp.exp(last), (SUBLANES, GLA_KEY)))

    ssd_v, ssd_qk, ssd_kv = [], [], []
    for c, r in enumerate(rws):
        v = xact[r, 0:BRANCH_WIDTH] * ssd_full[c][0:CHUNK]
        ssd_v.append(v)
        bm = xact[r, col_b]
        ssd_qk.append(_dot_nt(_pair_split_rows(_mx(xact[r, col_c]), lo_mask), _mx(bm)))
        kv = _dot(_mx(bm.T), _mx(v * ssd_full[c][2 * CHUNK:3 * CHUNK]))
        ssd_kv.append(jnp.where(ssd_diag, kv, 0.0))
    ret_qk, ret_kv = [], []
    for c, r in enumerate(rws):
        ks = retp[r, BRANCH_WIDTH:2 * BRANCH_WIDTH] * ret_wst_ref[...]
        for j, blk in enumerate(blks):
            q = retp[r, blk]
            k = retp[r, BRANCH_WIDTH + j * LANES:BRANCH_WIDTH + (j + 1) * LANES]
            vj = retp[r, 2 * BRANCH_WIDTH + j * LANES:2 * BRANCH_WIDTH + (j + 1) * LANES]
            ret_qk.append(_dot_nt(_pair_split_rows(_mx(q), lo_mask), _mx(k)))
            ret_kv.append(jnp.where(pair_diag, _dot(_mx(ks[:, blk].T), _mx(vj)), 0.0))
    gla_qs, gla_sc, gla_kv, gla_dec = [], [], [], []
    for c, r in enumerate(grw):
        for j in range(GLA_HEADS // 2):
            blk = blks[j]
            qs = _pair_split_rows(_mx(gla_q[c][:, blk]), lo_mask)
            gla_qs.append(qs)
            sc = _dot_nt(qs, gla_kinv[c][:, blk])
            gla_sc.append(_mx(jnp.where(gla_causal, sc, 0.0)))
            kst_t = _mx(gla_kst[c][:, blk].T)
            kvs = []
            for hh in range(2):
                hd = 2 * j + hh
                vh = _mx(glap[r, 2 * GLA_KEY + hd * LANES:2 * GLA_KEY + (hd + 1) * LANES])
                kvs.append(_dot(kst_t[hh * HALF:(hh + 1) * HALF], vh))
            gla_kv.append(jnp.concatenate(kvs, axis=0))
            gla_dec.append(gla_elast[c][:, blk].T[:, 0:1])

    ssd_intra = []
    for c, r in enumerate(rws):
        cum_t = cum[c].T
        ys = []
        for j in range(SSM_HEADS // 2):
            g = j // 2
            qkg = ssd_qk[c][g * CHUNK:(g + 1) * CHUNK, :]
            sc = []
            for hh in (2 * j, 2 * j + 1):
                seg = cum[c][:, hh:hh + 1] - cum_t[hh:hh + 1, :]
                sc.append(_mx(qkg * jnp.exp(jnp.where(causal, seg, NEG_BIG))))
            vj = ssd_v[c][:, blks[j]]
            ys.append(_dot(jnp.concatenate(sc, axis=1), _pair_split_rows(_mx(vj), lo_mask)))
        ssd_intra.append(jnp.concatenate(ys, axis=1))

    s_cur = ssd_state[...]
    for c, r in enumerate(rws):
        ecum_full = ssd_full[c][CHUNK:2 * CHUNK]
        yssm[r, :] = ssd_intra[c] + _dot(_mx(xact[r, col_c]), _mx(s_cur)) * ecum_full
        s_cur = s_cur * ecum_full[CHUNK - 1:CHUNK, :] + ssd_kv[c]
    ssd_state[...] = s_cur

    for j, blk in enumerate(blks):
        sj = ret_state[j]
        for c, r in enumerate(rws):
            sc = _mx(ret_qk[c * n_pairs + j] * ret_d_ref[j])
            qd = _mx(retp[r, blk] * ret_qdec_ref[:, blk])
            vj = retp[r, 2 * BRANCH_WIDTH + j * LANES:2 * BRANCH_WIDTH + (j + 1) * LANES]
            lhs = jnp.concatenate([sc[0:CHUNK], sc[CHUNK:2 * CHUNK], qd], axis=1)
            rhs = jnp.concatenate([_pair_split_rows(_mx(vj), lo_mask), _mx(sj)], axis=0)
            yret[r, blk] = _dot(lhs, rhs)
            sj = sj * ret_cdec_ref[:, blk] + ret_kv[c * n_pairs + j]
        ret_state[j] = sj

    for j in range(GLA_HEADS // 2):
        sp = gla_state[j]
        for c, r in enumerate(grw):
            i = c * (GLA_HEADS // 2) + j
            rhs_state = _mx(sp)
            for hh in range(2):
                hd = 2 * j + hh
                hr = slice(hh * GLA_CHUNK, (hh + 1) * GLA_CHUNK)
                vh = _mx(glap[r, 2 * GLA_KEY + hd * LANES:2 * GLA_KEY + (hd + 1) * LANES])
                lhs = jnp.concatenate([gla_qs[i][hr], gla_sc[i][hr]], axis=1)
                ygla[r, hd * LANES:(hd + 1) * LANES] = _dot(lhs, jnp.concatenate([rhs_state, vh], axis=0))
            sp = sp * gla_dec[i] + gla_kv[i]
        gla_state[j] = sp

    y = (yssm[...] + dfull_ref[...] * xact[:, 0:BRANCH_WIDTH]) * zs[...]
    parts = []
    gw = BRANCH_WIDTH // 2
    for g in range(2):
        yg = y[:, g * gw:(g + 1) * gw]
        parts.append(yg * lax.rsqrt(jnp.mean(yg * yg, axis=-1, keepdims=True) + GN_EPS))
    y_ssm = jnp.concatenate(parts, axis=1) * ssm_ng_ref[...]

    parts = []
    for j in range(RET_HEADS // 2):
        yb = yret[:, j * LANES:(j + 1) * LANES]
        s_all = jnp.sum(yb, axis=-1, keepdims=True)
        s_lo = jnp.sum(jnp.where(lo_mask, yb, 0.0), axis=-1, keepdims=True)
        yc = yb - jnp.where(lo_mask, s_lo, s_all - s_lo) * (1.0 / HALF)
        sq = yc * yc
        v_all = jnp.sum(sq, axis=-1, keepdims=True)
        v_lo = jnp.sum(jnp.where(lo_mask, sq, 0.0), axis=-1, keepdims=True)
        var = jnp.where(lo_mask, v_lo, v_all - v_lo) * (1.0 / HALF)
        parts.append(yc * lax.rsqrt(var + GN_EPS))
    y_ret = jnp.concatenate(parts, axis=1) * ret_ng_ref[...] * retp[:, 3 * BRANCH_WIDTH:]

    parts = []
    for hd in range(GLA_HEADS):
        yb = ygla[:, hd * LANES:(hd + 1) * LANES]
        parts.append(yb * lax.rsqrt(jnp.mean(yb * yb, axis=-1, keepdims=True) + GN_EPS))
    y_gla = jnp.concatenate(parts, axis=1) * gla_ng_ref[...] * glap[:, 2 * GLA_KEY + BRANCH_WIDTH:]

    h = h_s[...]
    merged = None
    for i, yb in enumerate((y_ssm, y_ret, y_gla)):
        cols = slice(i * D_MODEL, (i + 1) * D_MODEL)
        gate = _sigmoid(_dot(h, w_gate_ref[:, cols]) + bg_ref[:, cols])
        term = gate * _dot(_mx(yb), wb_ref[i])
        merged = term if merged is None else merged + term
    o_ref[...] = x_ref[...] + _dot(_mx(merged), wout_ref[...])


def _ffn_kernel(*refs, tm, final_norm):
    nlb = D_MODEL // LANES
    x_refs = refs[:nlb]
    ng_ref, wup_ref, cw_ref, cb_ref, wdown_ref, fg_ref, o_ref, hist_buf, a_s, row_s = refs[nlb:]
    t = pl.program_id(1)
    gr = tm // FFN_PHASES
    reg = SUBLANES + gr
    win = [slice(i * reg + SUBLANES - 1, i * reg + SUBLANES - 1 + gr) for i in range(2)]
    body = [slice(i * reg + SUBLANES, (i + 1) * reg) for i in range(2)]
    head = [slice(i * reg, i * reg + SUBLANES) for i in range(2)]
    tail = [slice((i + 1) * reg - SUBLANES, (i + 1) * reg) for i in range(2)]

    @pl.when(t == 0)
    def _():
        for i in range(2):
            hist_buf[head[i], :] = jnp.zeros((SUBLANES, 2 * D_FF), F32)

    x = jnp.concatenate(
        [jnp.concatenate([x_refs[c][pl.ds(p, gr, stride=FFN_PHASES), :] for c in range(nlb)], axis=1)
         for p in range(FFN_PHASES)], axis=0)
    h = _mx(_rms_norm(x, ng_ref[...]))
    for j in range(D_FF // FFN_BW):
        acts = []
        for part in range(2):
            cols = slice(part * D_FF + j * FFN_BW, part * D_FF + (j + 1) * FFN_BW)
            u = _dot(h, wup_ref[:, cols])
            cur = [u[p * gr:(p + 1) * gr] for p in range(FFN_PHASES)]
            for i in range(2):
                hist_buf[body[i], cols] = cur[FFN_PHASES - 2 + i]
            w2 = hist_buf[win[0], cols]
            w3 = hist_buf[win[1], cols]
            prev1 = [w3, cur[0], cur[1], cur[2]]
            prev2 = [w2, w3, cur[0], cur[1]]
            out = []
            for p in range(FFN_PHASES):
                out.append(cb_ref[:, cols] + cw_ref[2:3, cols] * cur[p] + cw_ref[1:2, cols] * prev1[p]
                           + cw_ref[0:1, cols] * prev2[p])
            acts.append(jnp.concatenate(out, axis=0))
        a_s[:, j * FFN_BW:(j + 1) * FFN_BW] = _mx(_silu(acts[0]) * acts[1])
    for i in range(2):
        hist_buf[head[i], :] = hist_buf[tail[i], :]
    ksplit = D_FF - 2 * FFN_BW
    y = x + _dot(a_s[:, :ksplit], wdown_ref[:ksplit, :]) + _dot(a_s[:, ksplit:], wdown_ref[ksplit:, :])
    if final_norm:
        y = _rms_norm(y, fg_ref[...])
    for c in range(nlb):
        for p in range(FFN_PHASES):
            row_s[c, pl.ds(p, gr, stride=FFN_PHASES), :] = y[p * gr:(p + 1) * gr, c * LANES:(c + 1) * LANES]
    for c in range(nlb):
        o_ref[:, c * LANES:(c + 1) * LANES] = row_s[c]


def _const_spec(shape):
    nd = len(shape)
    return pl.BlockSpec(shape, lambda b, t, _nd=nd: (0,) * _nd, pipeline_mode=pl.Buffered(1))


def _mixer_constants(seq):
    half = RET_HEAD_DIM // 2
    inv_freq = ROPE_BASE ** (-np.arange(half, dtype=np.float64) / half)
    ang = np.arange(seq, dtype=np.float64)[:, None] * inv_freq[None, :]
    lane = np.arange(LANES)
    cos = np.cos(ang)[:, lane % half]
    sin = np.sin(ang)[:, lane % half] * np.where((lane % RET_HEAD_DIM) < half, -1.0, 1.0)[None, :]

    log_gamma = np.log(1.0 - np.exp2(-5.0 - np.arange(RET_HEADS, dtype=np.float64)))
    pos = np.arange(CHUNK, dtype=np.float64)
    diff = pos[:, None] - pos[None, :]
    dmat = np.where(diff >= 0, np.exp(np.maximum(diff, 0.0)[None] * log_gamma[:, None, None]), 0.0)
    ret_d = dmat.reshape(RET_HEADS // 2, 2 * CHUNK, CHUNK)
    lg_full = np.repeat(log_gamma, RET_HEAD_DIM)
    ret_qdec = np.exp((pos[:, None] + 1.0) * lg_full[None, :])
    ret_wst = np.exp((CHUNK - 1.0 - pos[:, None]) * lg_full[None, :])
    ret_cdec = np.exp(CHUNK * lg_full)[None, :]

    expand = (np.arange(LANES)[:, None] == (np.arange(BRANCH_WIDTH)[None, :] // HALF)).astype(np.float64)
    e2 = np.concatenate([expand, expand], axis=0)
    tri = (pos[:, None] >= pos[None, :]).astype(np.float64)
    t2 = np.concatenate([tri, tri], axis=1)
    tri64 = tri[:GLA_CHUNK, :GLA_CHUNK]
    t64 = np.concatenate([tri64, tri64], axis=1)
    f = lambda a: jnp.asarray(a, F32)
    m = lambda a: jnp.asarray(a, MXU_DTYPE)
    return dict(cos=f(cos), sin=f(sin), ret_d=f(ret_d), ret_qdec=f(ret_qdec), ret_wst=f(ret_wst),
                ret_cdec=f(ret_cdec), e2=m(e2), t2=m(t2), t64=m(t64))


def _layer_spec(arr, layer):
    nd = arr.ndim - 1
    return pl.BlockSpec((None,) + arr.shape[1:], lambda b, t, _l=layer, _nd=nd: (_l,) + (0,) * _nd,
                        pipeline_mode=pl.Buffered(1))


def _rows(v):
    return v.reshape(v.shape[0], 1, -1).astype(F32)


def _pad_last(v, width):
    return jnp.pad(v, [(0, 0)] * (v.ndim - 1) + [(0, width - v.shape[-1])])


def _mixer_params(norm_mix_g, w_in, ssm_conv_w, ssm_conv_b, ssm_dt_bias, ssm_a_log, ssm_d, ssm_norm_g,
                  ret_norm_g, gla_w_alpha2, gla_b_alpha, gla_norm_g, w_branch, b_gate, w_out):
    w_all = jnp.concatenate([
        w_in[:, :, _OFF_RET:_OFF_GLA],
        _pad_last(w_in[:, :, _OFF_DT:_OFF_RET], LANES), _pad_last(w_in[:, :, _OFF_GLR:_OFF_GATE], LANES),
        jnp.zeros(w_in.shape[:2] + (_W_GATE_OFF - _W_SMALL_OFF - 2 * LANES,), w_in.dtype),
        w_in[:, :, _OFF_GATE:N_IN], w_in[:, :, _OFF_GLA:_OFF_GLR], w_in[:, :, _OFF_Z:_OFF_DT]], axis=-1).astype(MXU_DTYPE)
    wa2 = jnp.pad(gla_w_alpha2, ((0, 0), (0, LANES - GLA_RANK), (0, 0))).astype(MXU_DTYPE)
    depth = w_in.shape[0]
    return dict(
        ng=_rows(norm_mix_g), w_all=w_all,
        cw=ssm_conv_w.astype(F32), cb=_rows(ssm_conv_b), dtb=_pad_last(_rows(ssm_dt_bias), LANES),
        alog=_pad_last(_rows(ssm_a_log), LANES), dfull=_rows(jnp.repeat(ssm_d, HALF, axis=-1)),
        ssm_ng=_rows(ssm_norm_g), ret_ng=_rows(ret_norm_g), wa2=wa2, ba=_rows(gla_b_alpha),
        gla_ng=_rows(gla_norm_g), wb=w_branch.astype(MXU_DTYPE), bg=_rows(b_gate.reshape(depth, -1)),
        wout=w_out.astype(MXU_DTYPE))


def _mix_layer(x, consts, p, layer):
    bsz, seq, d = x.shape
    tm = MIX_TM
    assert seq % tm == 0 and tm % CHUNK == 0 and d == D_MODEL
    lay = lambda name: (p[name], _layer_spec(p[name], layer))

    def wseg(off, width):
        assert off % width == 0
        return (p["w_all"], pl.BlockSpec((None, d, width), lambda b, t, _l=layer, _i=off // width: (_l, 0, _i),
                                         pipeline_mode=pl.Buffered(1)))

    cst = lambda name: (consts[name], _const_spec(consts[name].shape))
    pos = lambda name: (consts[name], pl.BlockSpec((tm, LANES), lambda b, t: (t, 0)))
    operands = [
        (x, pl.BlockSpec((None, tm, d), lambda b, t: (b, t, 0))),
        lay("ng"), wseg(_W_SSM_OFF, _W_SSM), wseg(_W_SMALL_OFF, 2 * LANES), wseg(0, _W_RET),
        wseg(_W_GLA_OFF, _W_GLA), wseg(_W_GATE_OFF, _W_GATE),
        lay("cw"), lay("cb"), lay("dtb"), lay("alog"), lay("dfull"), lay("ssm_ng"),
        pos("cos"), pos("sin"), cst("ret_d"), cst("ret_qdec"), cst("ret_wst"), cst("ret_cdec"),
        lay("ret_ng"), lay("wa2"), lay("ba"), lay("gla_ng"), lay("wb"), lay("bg"), lay("wout"),
        cst("e2"), cst("t2"), cst("t64"),
    ]
    scratch = [
        pltpu.VMEM((tm, d), MXU_DTYPE),
        pltpu.VMEM((SUBLANES + tm, SSM_CONV_DIM), F32),
        pltpu.VMEM((tm, SSM_CONV_DIM), F32),
        pltpu.VMEM((tm, BRANCH_WIDTH), F32),
        pltpu.VMEM((tm, LANES), F32),
        pltpu.VMEM((tm, LANES), F32),
        pltpu.VMEM((tm, 4 * BRANCH_WIDTH), F32),
        pltpu.VMEM((tm, 2 * GLA_KEY + 2 * BRANCH_WIDTH), F32),
        pltpu.VMEM((tm, GLA_KEY), F32),
        pltpu.VMEM((tm, BRANCH_WIDTH), F32),
        pltpu.VMEM((tm, BRANCH_WIDTH), F32),
        pltpu.VMEM((tm, BRANCH_WIDTH), F32),
        pltpu.VMEM((CHUNK, BRANCH_WIDTH), F32),
        pltpu.VMEM((RET_HEADS // 2, LANES, LANES), F32),
        pltpu.VMEM((GLA_HEADS // 2, LANES, LANES), F32),
    ]
    return pl.pallas_call(
        functools.partial(_mix_kernel, tm=tm),
        grid=(bsz, seq // tm),
        in_specs=[s for _, s in operands],
        out_specs=pl.BlockSpec((None, tm, d), lambda b, t: (b, t, 0)),
        out_shape=jax.ShapeDtypeStruct(x.shape, x.dtype),
        scratch_shapes=scratch,
        compiler_params=pltpu.CompilerParams(dimension_semantics=("arbitrary", "arbitrary"),
                                             vmem_limit_bytes=VMEM_LIMIT_BYTES),
        name="mixer_layer",
    )(*[a for a, _ in operands])


def _ffn_params(norm_ffn_g, w_up, ffn_conv_w, ffn_conv_b, w_down):
    return dict(ng=_rows(norm_ffn_g), wup=w_up.astype(MXU_DTYPE), cw=ffn_conv_w.astype(F32),
                cb=_rows(ffn_conv_b), wdown=w_down.astype(MXU_DTYPE))


def _ffn_layer(x, p, final_g, layer, final_norm):
    bsz, seq, d = x.shape
    tm = FFN_TM
    assert seq % tm == 0 and d == D_MODEL and D_FF % FFN_BW == 0 and FFN_CONV == 3 and FFN_PHASES == 4
    final_g = final_g.reshape(1, -1).astype(F32)
    names = ("ng", "wup", "cw", "cb", "wdown")
    nlb = d // LANES
    args = [x] * nlb + [p[n] for n in names] + [final_g]
    specs = ([pl.BlockSpec((None, tm, LANES), lambda b, t, _c=c: (b, t, _c)) for c in range(nlb)]
             + [_layer_spec(p[n], layer) for n in names] + [_const_spec(final_g.shape)])
    return pl.pallas_call(
        functools.partial(_ffn_kernel, tm=tm, final_norm=final_norm),
        grid=(bsz, seq // tm),
        in_specs=specs,
        out_specs=pl.BlockSpec((None, tm, d), lambda b, t: (b, t, 0)),
        out_shape=jax.ShapeDtypeStruct(x.shape, x.dtype),
        scratch_shapes=[pltpu.VMEM((2 * (SUBLANES + tm // FFN_PHASES), 2 * D_FF), F32), pltpu.VMEM((tm, D_FF), MXU_DTYPE),
                        pltpu.VMEM((D_MODEL // LANES, tm, LANES), F32)],
        compiler_params=pltpu.CompilerParams(dimension_semantics=("arbitrary", "arbitrary"),
                                             vmem_limit_bytes=VMEM_LIMIT_BYTES),
        name="ffn_layer",
    )(*args)


def kernel(x, norm_mix_g, w_in, ssm_conv_w, ssm_conv_b, ssm_dt_bias, ssm_a_log, ssm_d, ssm_norm_g, ret_norm_g, gla_w_alpha2, gla_b_alpha, gla_norm_g, w_branch, b_gate, w_out, norm_ffn_g, w_up, ffn_conv_w, ffn_conv_b, w_down, norm_f_g):
    depth = w_in.shape[0]
    consts = _mixer_constants(x.shape[1])
    mix_p = _mixer_params(norm_mix_g, w_in, ssm_conv_w, ssm_conv_b, ssm_dt_bias, ssm_a_log, ssm_d, ssm_norm_g,
                          ret_norm_g, gla_w_alpha2, gla_b_alpha, gla_norm_g, w_branch, b_gate, w_out)
    ffn_p = _ffn_params(norm_ffn_g, w_up, ffn_conv_w, ffn_conv_b, w_down)
    for i in range(depth):
        x = _mix_layer(x, consts, mix_p, i)
        x = _ffn_layer(x, ffn_p, norm_f_g, i, final_norm=(i == depth - 1))
    return x
```

```python
import functools
import math

import numpy as np
import jax
import jax.numpy as jnp
from jax import lax
from jax.experimental import pallas as pl
from jax.experimental.pallas import tpu as pltpu

F32 = jnp.float32
MXU_DTYPE = jnp.bfloat16

D_MODEL = 1024
BRANCH_WIDTH = 512
N_BRANCH = 3
CHUNK = 128
GLA_CHUNK = 64
SSM_HEADS = 8
SSM_CONV = 4
SSM_CONV_DIM = 768
RET_HEADS = 8
RET_HEAD_DIM = 64
ROPE_BASE = 10000.0
GLA_HEADS = 4
GLA_KEY = 256
GLA_RANK = 16
GLA_GATE_NORMALIZER = 16.0
D_FF = 2816
FFN_CONV = 3
RMS_EPS = 1e-6
GN_EPS = 1e-5

_OFF_Z = 0
_OFF_XBC = 512
_OFF_DT = 1280
_OFF_RET = 1288
_OFF_GLA = 3336
_OFF_GLR = 4872
_OFF_GATE = 4888
N_IN = 7960
_W_RET = _OFF_GLA - _OFF_RET
_W_SMALL_OFF = _W_RET
_W_GATE = N_IN - _OFF_GATE
_W_GATE_OFF = _W_GATE
_W_GLA = _OFF_GLR - _OFF_GLA
_W_GLA_OFF = _W_GATE_OFF + _W_GATE
_W_SSM = _OFF_DT - _OFF_Z
_W_SSM_OFF = _W_GLA_OFF + _W_GLA

LANES = 128
SUBLANES = 8
HALF = 64
MIX_TM = 512
FFN_TM = 512
FFN_PHASES = 4
FFN_BW = 256
VMEM_LIMIT_BYTES = 60000 * 1024
NEG_BIG = -1e30


def _sigmoid(x):
    return 0.5 * jnp.tanh(0.5 * x) + 0.5


def _silu(x):
    hx = 0.5 * x
    return hx * (jnp.tanh(hx) + 1.0)


def _softplus(x):
    return jnp.maximum(x, 0.0) + jnp.log(1.0 + jnp.exp(-jnp.abs(x)))


def _split2(x):
    hi = x.astype(MXU_DTYPE)
    lo = (x - hi.astype(F32)).astype(MXU_DTYPE)
    return hi, lo


def _const_dot(c2, x):
    hi, lo = _split2(x)
    return jnp.dot(c2, jnp.concatenate([hi, lo], axis=0), preferred_element_type=F32)


def _dot_const(x, c2):
    hi, lo = _split2(x)
    return jnp.dot(jnp.concatenate([hi, lo], axis=1), c2, preferred_element_type=F32)


def _dot(a, b):
    return jnp.dot(a, b, preferred_element_type=F32)


def _dot_nt(a, b):
    return lax.dot_general(a, b, (((1,), (1,)), ((), ())), preferred_element_type=F32)


def _mx(x):
    return x.astype(MXU_DTYPE)


def _rms_norm(x, g):
    ms = jnp.mean(x * x, axis=-1, keepdims=True)
    return x * lax.rsqrt(ms + RMS_EPS) * g


def _rms_split(x, g):
    ms = jnp.mean(x * x, axis=-1, keepdims=True)
    return _mx(x * g), lax.rsqrt(ms + RMS_EPS)


def _pair_split_rows(x, lo_mask):
    zero = jnp.zeros_like(x)
    return jnp.concatenate([jnp.where(lo_mask, x, zero), jnp.where(lo_mask, zero, x)], axis=0)


def _mix_kernel(x_ref, ng_ref, w_ssm_ref, w_small_ref, w_ret_ref, w_gla_ref, w_gate_ref,
                cw_ref, cb_ref, dtb_ref, alog_ref, dfull_ref, ssm_ng_ref,
                cos_ref, sin_ref, ret_d_ref, ret_qdec_ref, ret_wst_ref, ret_cdec_ref, ret_ng_ref,
                wa2_ref, ba_ref, gla_ng_ref, wb_ref, bg_ref, wout_ref,
                e2_ref, t2_ref, t64_ref,
                o_ref,
                h_s, xbc_buf, xact, zs, dt_s, la_s, retp, glap, lag,
                yssm, yret, ygla, ssd_state, ret_state, gla_state, *, tm):
    t = pl.program_id(1)

    @pl.when(t == 0)
    def _():
        xbc_buf[0:SUBLANES, :] = jnp.zeros((SUBLANES, SSM_CONV_DIM), F32)
        ssd_state[...] = jnp.zeros_like(ssd_state)
        ret_state[...] = jnp.zeros_like(ret_state)
        gla_state[...] = jnp.zeros_like(gla_state)

    lane = lax.broadcasted_iota(jnp.int32, (1, LANES), 1)
    lo_mask = lane < HALF
    rot_first = (lane % HALF) < (HALF // 2)

    xg, rs = _rms_split(x_ref[...], ng_ref[...])
    h = _mx(x_ref[...] * rs * ng_ref[...])
    h_s[...] = h

    zx = _dot(xg, w_ssm_ref[...]) * rs
    zs[...] = _silu(zx[:, :BRANCH_WIDTH])
    xbc_buf[SUBLANES:SUBLANES + tm, :] = zx[:, BRANCH_WIDTH:]
    conv = cb_ref[...]
    for k in range(SSM_CONV):
        off = SUBLANES - (SSM_CONV - 1) + k
        conv = conv + cw_ref[k:k + 1, :] * xbc_buf[off:off + tm, :]
    xbc_buf[0:SUBLANES, :] = xbc_buf[tm:tm + SUBLANES, :]
    xact[...] = _silu(conv)

    small = _dot(h, w_small_ref[...])
    dt = _softplus(small[:, :LANES] + dtb_ref[...])
    dt_s[...] = dt
    la_s[...] = dt * (-jnp.exp(alog_ref[...]))

    rp = _dot(h, w_ret_ref[...])
    cos = cos_ref[...]
    sin = sin_ref[...]
    for j in range(2 * BRANCH_WIDTH // LANES):
        xb = rp[:, j * LANES:(j + 1) * LANES]
        partner = jnp.where(rot_first, pltpu.roll(xb, LANES - HALF // 2, 1), pltpu.roll(xb, HALF // 2, 1))
        rot = xb * cos + partner * sin
        if j >= BRANCH_WIDTH // LANES:
            rot = rot * (RET_HEAD_DIM ** -0.5)
        retp[:, j * LANES:(j + 1) * LANES] = rot
    retp[:, 2 * BRANCH_WIDTH:3 * BRANCH_WIDTH] = rp[:, 2 * BRANCH_WIDTH:3 * BRANCH_WIDTH]
    retp[:, 3 * BRANCH_WIDTH:] = _silu(rp[:, 3 * BRANCH_WIDTH:])

    gp = _dot(h, w_gla_ref[...])
    glap[:, 0:2 * GLA_KEY + BRANCH_WIDTH] = gp[:, 0:2 * GLA_KEY + BRANCH_WIDTH]
    glap[:, 2 * GLA_KEY + BRANCH_WIDTH:] = _silu(gp[:, 2 * GLA_KEY + BRANCH_WIDTH:])
    al = _dot(_mx(small[:, LANES:]), wa2_ref[...]) + ba_ref[...]
    lag[...] = (jnp.minimum(al, 0.0) - jnp.log(1.0 + jnp.exp(-jnp.abs(al)))) * (1.0 / GLA_GATE_NORMALIZER)

    row = lax.broadcasted_iota(jnp.int32, (CHUNK, CHUNK), 0)
    col = lax.broadcasted_iota(jnp.int32, (CHUNK, CHUNK), 1)
    causal = row >= col
    pair_diag = (row // HALF) == (col // HALF)
    srow = lax.broadcasted_iota(jnp.int32, (CHUNK, BRANCH_WIDTH), 0)
    scol = lax.broadcasted_iota(jnp.int32, (CHUNK, BRANCH_WIDTH), 1)
    ssd_diag = (srow // HALF) == (scol // (BRANCH_WIDTH // 2))
    grow = lax.broadcasted_iota(jnp.int32, (2 * GLA_CHUNK, GLA_CHUNK), 0)
    gcol = lax.broadcasted_iota(jnp.int32, (2 * GLA_CHUNK, GLA_CHUNK), 1)
    gla_causal = (grow % GLA_CHUNK) >= gcol

    nch = tm // CHUNK
    ngc = tm // GLA_CHUNK
    n_pairs = BRANCH_WIDTH // LANES
    rws = [slice(c * CHUNK, (c + 1) * CHUNK) for c in range(nch)]
    grw = [slice(c * GLA_CHUNK, (c + 1) * GLA_CHUNK) for c in range(ngc)]
    blks = [slice(j * LANES, (j + 1) * LANES) for j in range(n_pairs)]
    col_b = slice(BRANCH_WIDTH, BRANCH_WIDTH + LANES)
    col_c = slice(BRANCH_WIDTH + LANES, BRANCH_WIDTH + 2 * LANES)


    cum = [_const_dot(t2_ref[...], la_s[r, :]) for r in rws]
    cumg = [_const_dot(t64_ref[...], lag[r, :]) for r in grw]

    ssd_full = []
    for c, r in enumerate(rws):
        ecum = jnp.exp(cum[c])
        wst = jnp.exp(cum[c][CHUNK - 1:CHUNK, :] - cum[c])
        ssd_full.append(_dot_const(jnp.concatenate([dt_s[r, :], ecum, wst], axis=0), e2_ref[...]))
    gla_q, gla_kinv, gla_kst, gla_elast = [], [], [], []
    for c, r in enumerate(grw):
        last = cumg[c][GLA_CHUNK - 1:GLA_CHUNK, :]
        gk = glap[r, GLA_KEY:2 * GLA_KEY]
        gla_q.append(glap[r, 0:GLA_KEY] * (HALF ** -0.5) * jnp.exp(cumg[c]))
        gla_kinv.append(_mx(gk * jnp.exp(-cumg[c])))
        gla_kst.append(gk * jnp.exp(last - cumg[c]))
        gla_elast.append(jnp.broadcast_to(jnp.exp(last), (SUBLANES, GLA_KEY)))

    ssd_v, ssd_qk, ssd_kv = [], [], []
    for c, r in enumerate(rws):
        v = xact[r, 0:BRANCH_WIDTH] * ssd_full[c][0:CHUNK]
        ssd_v.append(v)
        bm = xact[r, col_b]
        ssd_qk.append(_dot_nt(_pair_split_rows(_mx(xact[r, col_c]), lo_mask), _mx(bm)))
        kv = _dot(_mx(bm.T), _mx(v * ssd_full[c][2 * CHUNK:3 * CHUNK]))
        ssd_kv.append(jnp.where(ssd_diag, kv, 0.0))
    ret_qk, ret_kv = [], []
    for c, r in enumerate(rws):
        ks = retp[r, BRANCH_WIDTH:2 * BRANCH_WIDTH] * ret_wst_ref[...]
        for j, blk in enumerate(blks):
            q = retp[r, blk]
            k = retp[r, BRANCH_WIDTH + j * LANES:BRANCH_WIDTH + (j + 1) * LANES]
            vj = retp[r, 2 * BRANCH_WIDTH + j * LANES:2 * BRANCH_WIDTH + (j + 1) * LANES]
            ret_qk.append(_dot_nt(_pair_split_rows(_mx(q), lo_mask), _mx(k)))
            ret_kv.append(jnp.where(pair_diag, _dot(_mx(ks[:, blk].T), _mx(vj)), 0.0))
    gla_qs, gla_sc, gla_kv, gla_dec = [], [], [], []
    for c, r in enumerate(grw):
        for j in range(GLA_HEADS // 2):
            blk = blks[j]
            qs = _pair_split_rows(_mx(gla_q[c][:, blk]), lo_mask)
            gla_qs.append(qs)
            sc = _dot_nt(qs, gla_kinv[c][:, blk])
            gla_sc.append(_mx(jnp.where(gla_causal, sc, 0.0)))
            kst_t = _mx(gla_kst[c][:, blk].T)
            kvs = []
            for hh in range(2):
                hd = 2 * j + hh
                vh = _mx(glap[r, 2 * GLA_KEY + hd * LANES:2 * GLA_KEY + (hd + 1) * LANES])
                kvs.append(_dot(kst_t[hh * HALF:(hh + 1) * HALF], vh))
            gla_kv.append(jnp.concatenate(kvs, axis=0))
            gla_dec.append(gla_elast[c][:, blk].T[:, 0:1])

    ssd_intra = []
    for c, r in enumerate(rws):
        cum_t = cum[c].T
        ys = []
        for j in range(SSM_HEADS // 2):
            g = j // 2
            qkg = ssd_qk[c][g * CHUNK:(g + 1) * CHUNK, :]
            sc = []
            for hh in (2 * j, 2 * j + 1):
                seg = cum[c][:, hh:hh + 1] - cum_t[hh:hh + 1, :]
                sc.append(_mx(qkg * jnp.exp(jnp.where(causal, seg, NEG_BIG))))
            vj = ssd_v[c][:, blks[j]]
            ys.append(_dot(jnp.concatenate(sc, axis=1), _pair_split_rows(_mx(vj), lo_mask)))
        ssd_intra.append(jnp.concatenate(ys, axis=1))

    s_cur = ssd_state[...]
    for c, r in enumerate(rws):
        ecum_full = ssd_full[c][CHUNK:2 * CHUNK]
        yssm[r, :] = ssd_intra[c] + _dot(_mx(xact[r, col_c]), _mx(s_cur)) * ecum_full
        s_cur = s_cur * ecum_full[CHUNK - 1:CHUNK, :] + ssd_kv[c]
    ssd_state[...] = s_cur

    for j, blk in enumerate(blks):
        sj = ret_state[j]
        for c, r in enumerate(rws):
            sc = _mx(ret_qk[c * n_pairs + j] * ret_d_ref[j])
            qd = _mx(retp[r, blk] * ret_qdec_ref[:, blk])
            vj = retp[r, 2 * BRANCH_WIDTH + j * LANES:2 * BRANCH_WIDTH + (j + 1) * LANES]
            lhs = jnp.concatenate([sc[0:CHUNK], sc[CHUNK:2 * CHUNK], qd], axis=1)
            rhs = jnp.concatenate([_pair_split_rows(_mx(vj), lo_mask), _mx(sj)], axis=0)
            yret[r, blk] = _dot(lhs, rhs)
            sj = sj * ret_cdec_ref[:, blk] + ret_kv[c * n_pairs + j]
        ret_state[j] = sj

    for j in range(GLA_HEADS // 2):
        sp = gla_state[j]
        for c, r in enumerate(grw):
            i = c * (GLA_HEADS // 2) + j
            rhs_state = _mx(sp)
            for hh in range(2):
                hd = 2 * j + hh
                hr = slice(hh * GLA_CHUNK, (hh + 1) * GLA_CHUNK)
                vh = _mx(glap[r, 2 * GLA_KEY + hd * LANES:2 * GLA_KEY + (hd + 1) * LANES])
                lhs = jnp.concatenate([gla_qs[i][hr], gla_sc[i][hr]], axis=1)
                ygla[r, hd * LANES:(hd + 1) * LANES] = _dot(lhs, jnp.concatenate([rhs_state, vh], axis=0))
            sp = sp * gla_dec[i] + gla_kv[i]
        gla_state[j] = sp

    y = (yssm[...] + dfull_ref[...] * xact[:, 0:BRANCH_WIDTH]) * zs[...]
    parts = []
    gw = BRANCH_WIDTH // 2
    for g in range(2):
        yg = y[:, g * gw:(g + 1) * gw]
        parts.append(yg * lax.rsqrt(jnp.mean(yg * yg, axis=-1, keepdims=True) + GN_EPS))
    y_ssm = jnp.concatenate(parts, axis=1) * ssm_ng_ref[...]

    parts = []
    for j in range(RET_HEADS // 2):
        yb = yret[:, j * LANES:(j + 1) * LANES]
        s_all = jnp.sum(yb, axis=-1, keepdims=True)
        s_lo = jnp.sum(jnp.where(lo_mask, yb, 0.0), axis=-1, keepdims=True)
        yc = yb - jnp.where(lo_mask, s_lo, s_all - s_lo) * (1.0 / HALF)
        sq = yc * yc
        v_all = jnp.sum(sq, axis=-1, keepdims=True)
        v_lo = jnp.sum(jnp.where(lo_mask, sq, 0.0), axis=-1, keepdims=True)
        var = jnp.where(lo_mask, v_lo, v_all - v_lo) * (1.0 / HALF)
        parts.append(yc * lax.rsqrt(var + GN_EPS))
    y_ret = jnp.concatenate(parts, axis=1) * ret_ng_ref[...] * retp[:, 3 * BRANCH_WIDTH:]

    parts = []
    for hd in range(GLA_HEADS):
        yb = ygla[:, hd * LANES:(hd + 1) * LANES]
        parts.append(yb * lax.rsqrt(jnp.mean(yb * yb, axis=-1, keepdims=True) + GN_EPS))
    y_gla = jnp.concatenate(parts, axis=1) * gla_ng_ref[...] * glap[:, 2 * GLA_KEY + BRANCH_WIDTH:]

    h = h_s[...]
    merged = None
    for i, yb in enumerate((y_ssm, y_ret, y_gla)):
        cols = slice(i * D_MODEL, (i + 1) * D_MODEL)
        gate = _sigmoid(_dot(h, w_gate_ref[:, cols]) + bg_ref[:, cols])
        term = gate * _dot(_mx(yb), wb_ref[i])
        merged = term if merged is None else merged + term
    o_ref[...] = x_ref[...] + _dot(_mx(merged), wout_ref[...])


def _ffn_kernel(*refs, tm, final_norm):
    nlb = D_MODEL // LANES
    x_refs = refs[:nlb]
    ng_ref, wup_ref, cw_ref, cb_ref, wdown_ref, fg_ref, o_ref, hist_buf, a_s, row_s = refs[nlb:]
    t = pl.program_id(1)
    gr = tm // FFN_PHASES
    reg = SUBLANES + gr
    win = [slice(i * reg + SUBLANES - 1, i * reg + SUBLANES - 1 + gr) for i in range(2)]
    body = [slice(i * reg + SUBLANES, (i + 1) * reg) for i in range(2)]
    head = [slice(i * reg, i * reg + SUBLANES) for i in range(2)]
    tail = [slice((i + 1) * reg - SUBLANES, (i + 1) * reg) for i in range(2)]

    @pl.when(t == 0)
    def _():
        for i in range(2):
            hist_buf[head[i], :] = jnp.zeros((SUBLANES, 2 * D_FF), F32)

    x = jnp.concatenate(
        [jnp.concatenate([x_refs[c][pl.ds(p, gr, stride=FFN_PHASES), :] for c in range(nlb)], axis=1)
         for p in range(FFN_PHASES)], axis=0)
    h, rs = _rms_split(x, ng_ref[...])
    for j in range(D_FF // FFN_BW):
        acts = []
        for part in range(2):
            cols = slice(part * D_FF + j * FFN_BW, part * D_FF + (j + 1) * FFN_BW)
            u = _dot(h, wup_ref[:, cols]) * rs
            cur = [u[p * gr:(p + 1) * gr] for p in range(FFN_PHASES)]
            for i in range(2):
                hist_buf[body[i], cols] = cur[FFN_PHASES - 2 + i]
            w2 = hist_buf[win[0], cols]
            w3 = hist_buf[win[1], cols]
            prev1 = [w3, cur[0], cur[1], cur[2]]
            prev2 = [w2, w3, cur[0], cur[1]]
            out = []
            for p in range(FFN_PHASES):
                out.append(cb_ref[:, cols] + cw_ref[2:3, cols] * cur[p] + cw_ref[1:2, cols] * prev1[p]
                           + cw_ref[0:1, cols] * prev2[p])
            acts.append(jnp.concatenate(out, axis=0))
        a_s[:, j * FFN_BW:(j + 1) * FFN_BW] = _mx(_silu(acts[0]) * acts[1])
    for i in range(2):
        hist_buf[head[i], :] = hist_buf[tail[i], :]
    ksplit = D_FF - 2 * FFN_BW
    y = x + _dot(a_s[:, :ksplit], wdown_ref[:ksplit, :]) + _dot(a_s[:, ksplit:], wdown_ref[ksplit:, :])
    if final_norm:
        y = _rms_norm(y, fg_ref[...])
    for c in range(nlb):
        for p in range(FFN_PHASES):
            row_s[c, pl.ds(p, gr, stride=FFN_PHASES), :] = y[p * gr:(p + 1) * gr, c * LANES:(c + 1) * LANES]
    for c in range(nlb):
        o_ref[:, c * LANES:(c + 1) * LANES] = row_s[c]


def _const_spec(shape):
    nd = len(shape)
    return pl.BlockSpec(shape, lambda b, t, _nd=nd: (0,) * _nd, pipeline_mode=pl.Buffered(1))


def _mixer_constants(seq):
    half = RET_HEAD_DIM // 2
    inv_freq = ROPE_BASE ** (-np.arange(half, dtype=np.float64) / half)
    ang = np.arange(seq, dtype=np.float64)[:, None] * inv_freq[None, :]
    lane = np.arange(LANES)
    cos = np.cos(ang)[:, lane % half]
    sin = np.sin(ang)[:, lane % half] * np.where((lane % RET_HEAD_DIM) < half, -1.0, 1.0)[None, :]

    log_gamma = np.log(1.0 - np.exp2(-5.0 - np.arange(RET_HEADS, dtype=np.float64)))
    pos = np.arange(CHUNK, dtype=np.float64)
    diff = pos[:, None] - pos[None, :]
    dmat = np.where(diff >= 0, np.exp(np.maximum(diff, 0.0)[None] * log_gamma[:, None, None]), 0.0)
    ret_d = dmat.reshape(RET_HEADS // 2, 2 * CHUNK, CHUNK)
    lg_full = np.repeat(log_gamma, RET_HEAD_DIM)
    ret_qdec = np.exp((pos[:, None] + 1.0) * lg_full[None, :])
    ret_wst = np.exp((CHUNK - 1.0 - pos[:, None]) * lg_full[None, :])
    ret_cdec = np.exp(CHUNK * lg_full)[None, :]

    expand = (np.arange(LANES)[:, None] == (np.arange(BRANCH_WIDTH)[None, :] // HALF)).astype(np.float64)
    e2 = np.concatenate([expand, expand], axis=0)
    tri = (pos[:, None] >= pos[None, :]).astype(np.float64)
    t2 = np.concatenate([tri, tri], axis=1)
    tri64 = tri[:GLA_CHUNK, :GLA_CHUNK]
    t64 = np.concatenate([tri64, tri64], axis=1)
    f = lambda a: jnp.asarray(a, F32)
    m = lambda a: jnp.asarray(a, MXU_DTYPE)
    return dict(cos=f(cos), sin=f(sin), ret_d=f(ret_d), ret_qdec=f(ret_qdec), ret_wst=f(ret_wst),
                ret_cdec=f(ret_cdec), e2=m(e2), t2=m(t2), t64=m(t64))


def _layer_spec(arr, layer):
    nd = arr.ndim - 1
    return pl.BlockSpec((None,) + arr.shape[1:], lambda b, t, _l=layer, _nd=nd: (_l,) + (0,) * _nd,
                        pipeline_mode=pl.Buffered(1))


def _rows(v):
    return v.reshape(v.shape[0], 1, -1).astype(F32)


def _pad_last(v, width):
    return jnp.pad(v, [(0, 0)] * (v.ndim - 1) + [(0, width - v.shape[-1])])


def _mixer_params(norm_mix_g, w_in, ssm_conv_w, ssm_conv_b, ssm_dt_bias, ssm_a_log, ssm_d, ssm_norm_g,
                  ret_norm_g, gla_w_alpha2, gla_b_alpha, gla_norm_g, w_branch, b_gate, w_out):
    w_all = jnp.concatenate([
        w_in[:, :, _OFF_RET:_OFF_GLA],
        _pad_last(w_in[:, :, _OFF_DT:_OFF_RET], LANES), _pad_last(w_in[:, :, _OFF_GLR:_OFF_GATE], LANES),
        jnp.zeros(w_in.shape[:2] + (_W_GATE_OFF - _W_SMALL_OFF - 2 * LANES,), w_in.dtype),
        w_in[:, :, _OFF_GATE:N_IN], w_in[:, :, _OFF_GLA:_OFF_GLR], w_in[:, :, _OFF_Z:_OFF_DT]], axis=-1).astype(MXU_DTYPE)
    wa2 = jnp.pad(gla_w_alpha2, ((0, 0), (0, LANES - GLA_RANK), (0, 0))).astype(MXU_DTYPE)
    depth = w_in.shape[0]
    return dict(
        ng=_rows(norm_mix_g), w_all=w_all,
        cw=ssm_conv_w.astype(F32), cb=_rows(ssm_conv_b), dtb=_pad_last(_rows(ssm_dt_bias), LANES),
        alog=_pad_last(_rows(ssm_a_log), LANES), dfull=_rows(jnp.repeat(ssm_d, HALF, axis=-1)),
        ssm_ng=_rows(ssm_norm_g), ret_ng=_rows(ret_norm_g), wa2=wa2, ba=_rows(gla_b_alpha),
        gla_ng=_rows(gla_norm_g), wb=w_branch.astype(MXU_DTYPE), bg=_rows(b_gate.reshape(depth, -1)),
        wout=w_out.astype(MXU_DTYPE))


def _mix_layer(x, consts, p, layer):
    bsz, seq, d = x.shape
    tm = MIX_TM
    assert seq % tm == 0 and tm % CHUNK == 0 and d == D_MODEL
    lay = lambda name: (p[name], _layer_spec(p[name], layer))

    def wseg(off, width):
        assert off % width == 0
        return (p["w_all"], pl.BlockSpec((None, d, width), lambda b, t, _l=layer, _i=off // width: (_l, 0, _i),
                                         pipeline_mode=pl.Buffered(1)))

    cst = lambda name: (consts[name], _const_spec(consts[name].shape))
    pos = lambda name: (consts[name], pl.BlockSpec((tm, LANES), lambda b, t: (t, 0)))
    operands = [
        (x, pl.BlockSpec((None, tm, d), lambda b, t: (b, t, 0))),
        lay("ng"), wseg(_W_SSM_OFF, _W_SSM), wseg(_W_SMALL_OFF, 2 * LANES), wseg(0, _W_RET),
        wseg(_W_GLA_OFF, _W_GLA), wseg(_W_GATE_OFF, _W_GATE),
        lay("cw"), lay("cb"), lay("dtb"), lay("alog"), lay("dfull"), lay("ssm_ng"),
        pos("cos"), pos("sin"), cst("ret_d"), cst("ret_qdec"), cst("ret_wst"), cst("ret_cdec"),
        lay("ret_ng"), lay("wa2"), lay("ba"), lay("gla_ng"), lay("wb"), lay("bg"), lay("wout"),
        cst("e2"), cst("t2"), cst("t64"),
    ]
    scratch = [
        pltpu.VMEM((tm, d), MXU_DTYPE),
        pltpu.VMEM((SUBLANES + tm, SSM_CONV_DIM), F32),
        pltpu.VMEM((tm, SSM_CONV_DIM), F32),
        pltpu.VMEM((tm, BRANCH_WIDTH), F32),
        pltpu.VMEM((tm, LANES), F32),
        pltpu.VMEM((tm, LANES), F32),
        pltpu.VMEM((tm, 4 * BRANCH_WIDTH), F32),
        pltpu.VMEM((tm, 2 * GLA_KEY + 2 * BRANCH_WIDTH), F32),
        pltpu.VMEM((tm, GLA_KEY), F32),
        pltpu.VMEM((tm, BRANCH_WIDTH), F32),
        pltpu.VMEM((tm, BRANCH_WIDTH), F32),
        pltpu.VMEM((tm, BRANCH_WIDTH), F32),
        pltpu.VMEM((CHUNK, BRANCH_WIDTH), F32),
        pltpu.VMEM((RET_HEADS // 2, LANES, LANES), F32),
        pltpu.VMEM((GLA_HEADS // 2, LANES, LANES), F32),
    ]
    return pl.pallas_call(
        functools.partial(_mix_kernel, tm=tm),
        grid=(bsz, seq // tm),
        in_specs=[s for _, s in operands],
        out_specs=pl.BlockSpec((None, tm, d), lambda b, t: (b, t, 0)),
        out_shape=jax.ShapeDtypeStruct(x.shape, x.dtype),
        scratch_shapes=scratch,
        compiler_params=pltpu.CompilerParams(dimension_semantics=("arbitrary", "arbitrary"),
                                             vmem_limit_bytes=VMEM_LIMIT_BYTES),
        name="mixer_layer",
    )(*[a for a, _ in operands])


def _ffn_params(norm_ffn_g, w_up, ffn_conv_w, ffn_conv_b, w_down):
    return dict(ng=_rows(norm_ffn_g), wup=w_up.astype(MXU_DTYPE), cw=ffn_conv_w.astype(F32),
                cb=_rows(ffn_conv_b), wdown=w_down.astype(MXU_DTYPE))


def _ffn_layer(x, p, final_g, layer, final_norm):
    bsz, seq, d = x.shape
    tm = FFN_TM
    assert seq % tm == 0 and d == D_MODEL and D_FF % FFN_BW == 0 and FFN_CONV == 3 and FFN_PHASES == 4
    final_g = final_g.reshape(1, -1).astype(F32)
    names = ("ng", "wup", "cw", "cb", "wdown")
    nlb = d // LANES
    args = [x] * nlb + [p[n] for n in names] + [final_g]
    specs = ([pl.BlockSpec((None, tm, LANES), lambda b, t, _c=c: (b, t, _c)) for c in range(nlb)]
             + [_layer_spec(p[n], layer) for n in names] + [_const_spec(final_g.shape)])
    return pl.pallas_call(
        functools.partial(_ffn_kernel, tm=tm, final_norm=final_norm),
        grid=(bsz, seq // tm),
        in_specs=specs,
        out_specs=pl.BlockSpec((None, tm, d), lambda b, t: (b, t, 0)),
        out_shape=jax.ShapeDtypeStruct(x.shape, x.dtype),
        scratch_shapes=[pltpu.VMEM((2 * (SUBLANES + tm // FFN_PHASES), 2 * D_FF), F32), pltpu.VMEM((tm, D_FF), MXU_DTYPE),
                        pltpu.VMEM((D_MODEL // LANES, tm, LANES), F32)],
        compiler_params=pltpu.CompilerParams(dimension_semantics=("arbitrary", "arbitrary"),
                                             vmem_limit_bytes=VMEM_LIMIT_BYTES),
        name="ffn_layer",
    )(*args)


def kernel(x, norm_mix_g, w_in, ssm_conv_w, ssm_conv_b, ssm_dt_bias, ssm_a_log, ssm_d, ssm_norm_g, ret_norm_g, gla_w_alpha2, gla_b_alpha, gla_norm_g, w_branch, b_gate, w_out, norm_ffn_g, w_up, ffn_conv_w, ffn_conv_b, w_down, norm_f_g):
    depth = w_in.shape[0]
    consts = _mixer_constants(x.shape[1])
    mix_p = _mixer_params(norm_mix_g, w_in, ssm_conv_w, ssm_conv_b, ssm_dt_bias, ssm_a_log, ssm_d, ssm_norm_g,
                          ret_norm_g, gla_w_alpha2, gla_b_alpha, gla_norm_g, w_branch, b_gate, w_out)
    ffn_p = _ffn_params(norm_ffn_g, w_up, ffn_conv_w, ffn_conv_b, w_down)
    for i in range(depth):
        x = _mix_layer(x, consts, mix_p, i)
        x = _ffn_layer(x, ffn_p, norm_f_g, i, final_norm=(i == depth - 1))
    return x
```
